```python
import math
import jax, jax.numpy as jnp
from jax import lax
import numpy as np

D_MODEL = 4096
BATCH = 4
SEQ = 4096
DEPTH = 1

CHUNK = 64
N_META = 16
HEAD_DIM = 128
H_FOX = D_MODEL // (2 * HEAD_DIM)
H_DSA = D_MODEL // (2 * HEAD_DIM)
W_FOX = H_FOX * HEAD_DIM
W_DSA = H_DSA * HEAD_DIM
H_IDX = 32
D_IDX = 64
TOPK_MAX = 256
D_FF = 4 * D_MODEL
N_BUCKETS = 32
MAX_DISTANCE = 128
FOX_BLOCK = 128
DSA_BLOCK = 32
N_BRANCH = 2
RMS_EPS = 1e-6
IN_SPLITS = (W_FOX, W_FOX, W_FOX, H_FOX, W_DSA, W_DSA, W_DSA, H_IDX * D_IDX, D_IDX, H_IDX, N_BRANCH * D_MODEL)
D_IN = 3 * W_FOX + H_FOX + 3 * W_DSA + H_IDX * D_IDX + D_IDX + H_IDX + N_BRANCH * D_MODEL

kernel_name = "gated_fox_dsa_hybrid_block"


def rmsnorm(x, g):
    xf = x.astype(jnp.float32)
    y = xf * lax.rsqrt(jnp.mean(xf * xf, axis=-1, keepdims=True) + RMS_EPS)
    return (y * g.astype(jnp.float32)).astype(x.dtype)


def chunk_ids(n):
    p = jnp.arange(n)
    return jnp.where(p < N_META, 0, (p - N_META) // CHUNK + 1)


def t5_bucket(rel):
    half = N_BUCKETS // 2
    max_exact = half // 2
    ret = jnp.where(rel > 0, half, 0)
    n = jnp.abs(rel)
    nf = jnp.maximum(n, 1).astype(jnp.float32)
    large = max_exact + (jnp.log(nf / max_exact) / math.log(MAX_DISTANCE / max_exact)
                         * (half - max_exact)).astype(jnp.int32)
    large = jnp.minimum(large, half - 1)
    return ret + jnp.where(n < max_exact, n, large)


def forgetting_attention(q, k, v, log_f):
    B, T, H, hd = q.shape
    dcum = jnp.cumsum(log_f, axis=1).transpose(0, 2, 1)
    scale = hd ** -0.5
    outs = []
    for start in range(0, T, FOX_BLOCK):
        end = start + FOX_BLOCK
        s = jnp.einsum('bqhd,bkhd->bhqk', q[:, start:end], k[:, :end],
                       preferred_element_type=jnp.float32) * scale
        decay = dcum[:, :, start:end, None] - dcum[:, :, None, :end]
        qpos = jnp.arange(start, end)[:, None]
        kpos = jnp.arange(end)[None, :]
        s = jnp.where(kpos <= qpos, s + decay, -jnp.inf)
        p = jax.nn.softmax(s, axis=-1).astype(v.dtype)
        outs.append(jnp.einsum('bhqk,bkhd->bqhd', p, v[:, :end]))
    return jnp.concatenate(outs, axis=1)


def dsa_attention(q, k, v, q_idx, k_idx, w_idx, rel_bias, k_top):
    B, T, H, hd = q.shape
    nblk = T // DSA_BLOCK
    cid = chunk_ids(T)
    bidx = jnp.arange(B)[:, None, None]

    def to_blocks(a):
        return a.reshape(B, nblk, DSA_BLOCK, *a.shape[2:]).swapaxes(0, 1)

    def one_block(args):
        qb, qib, wib, start = args
        qpos = start + jnp.arange(DSA_BLOCK)
        dots = jnp.einsum('bqhd,bsd->bqhs', qib, k_idx,
                          preferred_element_type=jnp.float32) * (D_IDX ** -0.5)
        score = jnp.einsum('bqh,bqhs->bqs', wib.astype(jnp.float32) * (H_IDX ** -0.5),
                           jax.nn.relu(dots))
        admissible = cid[None, :] <= cid[qpos][:, None]
        score = jnp.where(admissible[None], score, -jnp.inf)
        top_val, top_idx = lax.top_k(score, k_top)
        valid = jnp.isfinite(top_val)
        ks = k[bidx, top_idx]
        vs = v[bidx, top_idx]
        s = jnp.einsum('bqhd,bqkhd->bhqk', qb, ks,
                       preferred_element_type=jnp.float32) * (hd ** -0.5)
        bucket = t5_bucket(top_idx - qpos[None, :, None])
        bias = rel_bias[bucket].astype(jnp.float32).transpose(0, 3, 1, 2)
        s = jnp.where(valid[:, None], s + bias, -jnp.inf)
        p = jax.nn.softmax(s, axis=-1).astype(v.dtype)
        return jnp.einsum('bhqk,bqkhd->bqhd', p, vs)

    starts = jnp.arange(nblk, dtype=jnp.int32) * DSA_BLOCK
    out = lax.map(one_block, (to_blocks(q), to_blocks(q_idx), to_blocks(w_idx), starts))
    return out.swapaxes(0, 1).reshape(B, T, H, hd)


def setup_inputs(seed: int = 0) -> dict:
    key = jax.random.key(seed)
    ks = jax.random.split(key, 14)
    f32 = jnp.float32
    nrm = lambda k, shape, s: jax.random.normal(k, shape, f32) * s
    return {
        "x": nrm(ks[0], (BATCH, SEQ, D_MODEL), 1.0),
        "meta_tokens": nrm(ks[1], (N_META, D_MODEL), 1.0),
        "attn_norm_g": 1.0 + nrm(ks[2], (DEPTH, D_MODEL), 0.02),
        "w_in": nrm(ks[3], (DEPTH, D_MODEL, D_IN), D_MODEL ** -0.5),
        "forget_bias": 2.0 + nrm(ks[4], (DEPTH, H_FOX), 0.1),
        "rel_bias": nrm(ks[5], (N_BUCKETS, H_DSA), 0.2),
        "w_branch_fox": nrm(ks[6], (DEPTH, W_FOX, D_MODEL), W_FOX ** -0.5),
        "w_branch_dsa": nrm(ks[7], (DEPTH, W_DSA, D_MODEL), W_DSA ** -0.5),
        "w_out": nrm(ks[8], (DEPTH, D_MODEL, D_MODEL), D_MODEL ** -0.5),
        "mlp_norm_g": 1.0 + nrm(ks[9], (DEPTH, D_MODEL), 0.02),
        "w_up": nrm(ks[10], (DEPTH, D_MODEL, D_FF), D_MODEL ** -0.5),
        "w_down": nrm(ks[11], (DEPTH, D_FF, D_MODEL), D_FF ** -0.5),
        "final_norm_g": 1.0 + nrm(ks[12], (D_MODEL,), 0.02),
    }


def reference(x, meta_tokens, attn_norm_g, w_in, forget_bias, rel_bias, w_branch_fox, w_branch_dsa,
              w_out, mlp_norm_g, w_up, w_down, final_norm_g):
    B, L, D = x.shape
    T = N_META + L
    T_pad = -(-T // FOX_BLOCK) * FOX_BLOCK
    meta = jnp.broadcast_to(meta_tokens[None].astype(x.dtype), (B, N_META, D))
    h = jnp.concatenate([meta, x, jnp.zeros((B, T_pad - T, D), x.dtype)], axis=1)
    k_top = min(TOPK_MAX, L // 4)
    split_at = np.cumsum(IN_SPLITS)[:-1].tolist()

    for layer in range(DEPTH):
        u = rmsnorm(h, attn_norm_g[layer])
        proj = u @ w_in[layer]
        qa, ka, va, fa, qb, kb, vb, qi, ki, wi, gl = jnp.split(proj, split_at, axis=-1)
        heads = lambda a, n: a.reshape(B, T_pad, n, -1)
        log_f = jax.nn.log_sigmoid(fa.astype(jnp.float32) + forget_bias[layer].astype(jnp.float32))
        o_fox = forgetting_attention(heads(qa, H_FOX), heads(ka, H_FOX), heads(va, H_FOX), log_f)
        o_dsa = dsa_attention(heads(qb, H_DSA), heads(kb, H_DSA), heads(vb, H_DSA),
                              heads(qi, H_IDX), ki, wi, rel_bias, k_top)
        y_fox = o_fox.reshape(B, T_pad, W_FOX) @ w_branch_fox[layer]
        y_dsa = o_dsa.reshape(B, T_pad, W_DSA) @ w_branch_dsa[layer]
        g = jax.nn.sigmoid(gl.astype(jnp.float32)).astype(h.dtype).reshape(B, T_pad, N_BRANCH, D)
        mixed = g[:, :, 0] * y_fox + g[:, :, 1] * y_dsa
        h = h + mixed @ w_out[layer]
        u = rmsnorm(h, mlp_norm_g[layer])
        h = h + jnp.square(jax.nn.relu(u @ w_up[layer])) @ w_down[layer]

    out = rmsnorm(h, final_norm_g)
    return out[:, N_META:N_META + L]
```

```python
import functools
import math

import jax
import jax.numpy as jnp
from jax import lax
from jax.experimental import pallas as pl
from jax.experimental.pallas import tpu as pltpu

F32 = jnp.float32
BF16 = jnp.bfloat16

CHUNK = 64
H_IDX = 32
D_IDX = 64
TOPK_MAX = 256
MAX_DISTANCE = 128
RMS_EPS = 1e-6

LANES = 128
META_PAD = 128
NEG = -1e30
INT_MIN = -(2 ** 31)
KEY_NEG_INF = -2139095041
VMEM_LIMIT = 56 * 1024 * 1024


def _tile(dim, pref):
    if dim <= pref:
        return dim
    for t in range(pref - pref % LANES, 0, -LANES):
        if dim % t == 0:
            return t
    raise ValueError((dim, pref))


def _cparams(*sem):
    return pltpu.CompilerParams(dimension_semantics=sem, vmem_limit_bytes=VMEM_LIMIT)


def _rmsnorm_kernel(x_ref, g_ref, o_ref):
    x = x_ref[...].astype(F32)
    ms = jnp.mean(x * x, axis=-1, keepdims=True)
    y = x * lax.rsqrt(ms + RMS_EPS)
    o_ref[...] = (y * g_ref[...]).astype(o_ref.dtype)


def _rmsnorm(x, g, out_dtype):
    m, d = x.shape
    tr = _tile(m, 256)
    return pl.pallas_call(
        _rmsnorm_kernel,
        grid=(m // tr,),
        in_specs=[pl.BlockSpec((tr, d), lambda i: (i, 0)), pl.BlockSpec((1, d), lambda i: (0, 0))],
        out_specs=pl.BlockSpec((tr, d), lambda i: (i, 0)),
        out_shape=jax.ShapeDtypeStruct((m, d), out_dtype),
        compiler_params=_cparams("arbitrary"),
        name="rmsnorm",
    )(x, g.reshape(1, d).astype(F32))


def _act(acc, act):
    if act == "sigmoid":
        return jax.nn.sigmoid(acc)
    if act == "relu2":
        r = jnp.maximum(acc, 0.0)
        return r * r
    return acc


def _mm_kernel(a_ref, b_ref, o_ref, *, act):
    acc = jnp.dot(a_ref[...], b_ref[...], preferred_element_type=F32)
    o_ref[...] = _act(acc, act).astype(o_ref.dtype)


def _mm_res_kernel(a_ref, b_ref, r_ref, o_ref):
    acc = jnp.dot(a_ref[...], b_ref[...], preferred_element_type=F32)
    o_ref[...] = (r_ref[...] + acc).astype(o_ref.dtype)


def _mm(a, b, out_dtype, act=None, res=None, tm=1024, tn=512):
    m, k = a.shape
    _, n = b.shape
    tm, tn = _tile(m, tm), _tile(n, tn)
    in_specs = [pl.BlockSpec((tm, k), lambda i, j: (i, 0)), pl.BlockSpec((k, tn), lambda i, j: (0, j))]
    args = [a, b]
    if res is None:
        body = functools.partial(_mm_kernel, act=act)
    else:
        body = _mm_res_kernel
        in_specs.append(pl.BlockSpec((tm, tn), lambda i, j: (i, j)))
        args.append(res)
    return pl.pallas_call(
        body,
        grid=(m // tm, n // tn),
        in_specs=in_specs,
        out_specs=pl.BlockSpec((tm, tn), lambda i, j: (i, j)),
        out_shape=jax.ShapeDtypeStruct((m, n), out_dtype),
        compiler_params=_cparams("arbitrary", "arbitrary"),
        name="matmul",
    )(*args)


def _mm_acc_kernel(a_ref, b_ref, r_ref, o_ref, acc_ref):
    kk = pl.program_id(2)

    @pl.when(kk == 0)
    def _():
        acc_ref[...] = jnp.zeros_like(acc_ref)

    acc_ref[...] += jnp.dot(a_ref[...], b_ref[...], preferred_element_type=F32)

    @pl.when(kk == pl.num_programs(2) - 1)
    def _():
        o_ref[...] = (r_ref[...] + acc_ref[...]).astype(o_ref.dtype)


def _mm_acc(a, b, res, out_dtype, tm=1024, tn=1024, tk=2048):
    m, k = a.shape
    _, n = b.shape
    tm, tn, tk = _tile(m, tm), _tile(n, tn), _tile(k, tk)
    return pl.pallas_call(
        _mm_acc_kernel,
        grid=(m // tm, n // tn, k // tk),
        in_specs=[
            pl.BlockSpec((tm, tk), lambda i, j, kk: (i, kk)),
            pl.BlockSpec((tk, tn), lambda i, j, kk: (kk, j)),
            pl.BlockSpec((tm, tn), lambda i, j, kk: (i, j)),
        ],
        out_specs=pl.BlockSpec((tm, tn), lambda i, j, kk: (i, j)),
        out_shape=jax.ShapeDtypeStruct((m, n), out_dtype),
        scratch_shapes=[pltpu.VMEM((tm, tn), F32)],
        compiler_params=_cparams("arbitrary", "arbitrary", "arbitrary"),
        name="matmul_acc",
    )(a, b, res)


def _branch_kernel(of_ref, od_ref, wf_ref, wd_ref, g0_ref, g1_ref, o_ref):
    yf = jnp.dot(of_ref[...], wf_ref[...], preferred_element_type=F32)
    yd = jnp.dot(od_ref[...], wd_ref[...], preferred_element_type=F32)
    o_ref[...] = (g0_ref[...] * yf + g1_ref[...] * yd).astype(o_ref.dtype)


def _branch_merge(o_fox, o_dsa, wf, wd, gate, out_dtype, tm=512, tn=1024):
    m, kf = o_fox.shape
    _, kd = o_dsa.shape
    n = wf.shape[1]
    tm, tn = _tile(m, tm), _tile(n, tn)
    nj = n // tn
    return pl.pallas_call(
        _branch_kernel,
        grid=(m // tm, nj),
        in_specs=[
            pl.BlockSpec((tm, kf), lambda i, j: (i, 0)),
            pl.BlockSpec((tm, kd), lambda i, j: (i, 0)),
            pl.BlockSpec((kf, tn), lambda i, j: (0, j)),
            pl.BlockSpec((kd, tn), lambda i, j: (0, j)),
            pl.BlockSpec((tm, tn), lambda i, j: (i, j)),
            pl.BlockSpec((tm, tn), lambda i, j: (i, j + nj)),
        ],
        out_specs=pl.BlockSpec((tm, tn), lambda i, j: (i, j)),
        out_shape=jax.ShapeDtypeStruct((m, n), out_dtype),
        compiler_params=_cparams("arbitrary", "arbitrary"),
        name="branch_merge",
    )(o_fox, o_dsa, wf, wd, gate, gate)


def _split3(x):
    hi = x.astype(BF16)
    r1 = x - hi.astype(F32)
    mid = r1.astype(BF16)
    lo = (r1 - mid.astype(F32)).astype(BF16)
    return hi, mid, lo


def _cumsum_lanes(xt, upper):
    out = None
    for part in _split3(xt):
        d = jnp.dot(part, upper, preferred_element_type=F32)
        out = d if out is None else out + d
    return out


def _decay_kernel(slab_ref, slabm_ref, bias_ref, dt_ref, dtm_ref, *, blk, lane0, nh, n_meta):
    L = slab_ref.shape[1]
    bias = bias_ref[...]

    def upper(n):
        r = lax.broadcasted_iota(jnp.int32, (n, n), 0)
        c = lax.broadcasted_iota(jnp.int32, (n, n), 1)
        return (r <= c).astype(BF16)

    xm = jax.nn.log_sigmoid(slabm_ref[...] + bias)
    dm = _cumsum_lanes(xm.T, upper(META_PAD))
    dtm_ref[0] = dm[lane0:lane0 + nh, :]
    carry = dm[:, n_meta - 1:n_meta]
    up = upper(blk)
    for i in range(L // blk):
        x = jax.nn.log_sigmoid(slab_ref[0, i * blk:(i + 1) * blk, :] + bias)
        d = _cumsum_lanes(x.T, up) + carry
        dt_ref[0, :, i * blk:(i + 1) * blk] = d[lane0:lane0 + nh, :]
        carry = d[:, blk - 1:blk]


def _decay(slab, slabm, bias_row, lane0, nh, n_meta):
    b, L, _ = slab.shape
    blk = _tile(L, 256)
    return pl.pallas_call(
        functools.partial(_decay_kernel, blk=blk, lane0=lane0, nh=nh, n_meta=n_meta),
        grid=(b,),
        in_specs=[
            pl.BlockSpec((1, L, LANES), lambda i: (i, 0, 0)),
            pl.BlockSpec((META_PAD, LANES), lambda i: (0, 0)),
            pl.BlockSpec((1, LANES), lambda i: (0, 0)),
        ],
        out_specs=[
            pl.BlockSpec((1, nh, L), lambda i: (i, 0, 0)),
            pl.BlockSpec((1, nh, META_PAD), lambda i: (i, 0, 0)),
        ],
        out_shape=[
            jax.ShapeDtypeStruct((b, nh, L), F32),
            jax.ShapeDtypeStruct((b, nh, META_PAD), F32),
        ],
        compiler_params=_cparams("arbitrary"),
        name="forget_cumsum",
    )(slab, slabm, bias_row)


def _dot_nt(a, b):
    return lax.dot_general(a, b, (((1,), (1,)), ((), ())), preferred_element_type=F32)


def _online_softmax_step(h, hd, s, v, m_ref, l_ref, acc_ref):
    ts = s.shape[1]
    m_prev = m_ref[h]
    m_new = jnp.maximum(m_prev, jnp.max(s, axis=1, keepdims=True))
    alpha = jnp.exp(m_prev - m_new)
    p = jnp.exp(s - jnp.tile(m_new, (1, ts // LANES)))
    l_ref[h] = alpha * l_ref[h] + jnp.sum(p, axis=1, keepdims=True)
    pv = jnp.dot(p.astype(BF16), v, preferred_element_type=F32)
    hs = slice(h * hd, (h + 1) * hd)
    acc_ref[:, hs] = acc_ref[:, hs] * jnp.tile(alpha, (1, hd // LANES)) + pv
    m_ref[h] = m_new


def _attn_init(m_ref, l_ref, acc_ref):
    m_ref[...] = jnp.full_like(m_ref, NEG)
    l_ref[...] = jnp.zeros_like(l_ref)
    acc_ref[...] = jnp.zeros_like(acc_ref)


def _attn_finish(o_ref, l_ref, acc_ref, nh, hd):
    for h in range(nh):
        hs = slice(h * hd, (h + 1) * hd)
        o_ref[0, :, hs] = (acc_ref[:, hs] / jnp.tile(l_ref[h], (1, hd // LANES))).astype(o_ref.dtype)


def _fox_kernel(q_ref, k_ref, v_ref, km_ref, vm_ref, dt_ref, dtm_ref, o_ref, m_ref, l_ref, acc_ref,
                *, nh, hd, n_meta):
    qt, kt = pl.program_id(1), pl.program_id(2)
    tq, ts = q_ref.shape[1], k_ref.shape[1]
    scale = hd ** -0.5

    @pl.when(kt == 0)
    def _():
        _attn_init(m_ref, l_ref, acc_ref)
        col = lax.broadcasted_iota(jnp.int32, (tq, META_PAD), 1)
        pad = jnp.where(col < n_meta, 0.0, NEG).astype(F32)
        for h in range(nh):
            hs = slice(h * hd, (h + 1) * hd)
            s = _dot_nt(q_ref[0, :, hs], km_ref[:, hs]) * scale - dtm_ref[0, h:h + 1, :] + pad
            _online_softmax_step(h, hd, s, vm_ref[:, hs], m_ref, l_ref, acc_ref)

    def frames(diag):
        if diag:
            row = lax.broadcasted_iota(jnp.int32, (tq, ts), 0)
            col = lax.broadcasted_iota(jnp.int32, (tq, ts), 1)
            causal = col <= row
        for h in range(nh):
            hs = slice(h * hd, (h + 1) * hd)
            s = _dot_nt(q_ref[0, :, hs], k_ref[0, :, hs]) * scale - dt_ref[0, h:h + 1, :]
            if diag:
                s = jnp.where(causal, s, NEG)
            _online_softmax_step(h, hd, s, v_ref[0, :, hs], m_ref, l_ref, acc_ref)

    @pl.when(kt < qt)
    def _():
        frames(False)

    @pl.when(kt == qt)
    def _():
        frames(True)
        _attn_finish(o_ref, l_ref, acc_ref, nh, hd)


def _dsa_kernel(q_ref, k_ref, v_ref, km_ref, vm_ref, mask_ref, maskm_ref, band_ref, o_ref,
                m_ref, l_ref, acc_ref, s_ref, *, nh, hd):
    qt, kt = pl.program_id(1), pl.program_id(2)
    tq, ts = q_ref.shape[1], k_ref.shape[1]
    nb = tq // LANES
    scale = hd ** -0.5

    @pl.when(kt == 0)
    def _():
        _attn_init(m_ref, l_ref, acc_ref)
        maskf = maskm_ref[0].astype(F32)
        for h in range(nh):
            hs = slice(h * hd, (h + 1) * hd)
            s_ref[:, :META_PAD] = _dot_nt(q_ref[0, :, hs], km_ref[:, hs]) * scale + maskf

            @pl.when(qt == 0)
            def _():
                s_ref[:LANES, :META_PAD] += band_ref[2, h]

            _online_softmax_step(h, hd, s_ref[:, :META_PAD], vm_ref[:, hs], m_ref, l_ref, acc_ref)

    @pl.when(kt <= qt)
    def _():
        maskf = mask_ref[0].astype(F32)
        for h in range(nh):
            hs = slice(h * hd, (h + 1) * hd)
            s_ref[...] = _dot_nt(q_ref[0, :, hs], k_ref[0, :, hs]) * scale + maskf

            @pl.when(kt == qt)
            def _():
                for ib in range(nb):
                    rows = slice(ib * LANES, (ib + 1) * LANES)
                    s_ref[rows, rows] += band_ref[0, h]
                    if ib > 0:
                        s_ref[rows, (ib - 1) * LANES:ib * LANES] += band_ref[1, h]

            @pl.when(kt == qt - 1)
            def _():
                s_ref[:LANES, ts - LANES:] += band_ref[1, h]

            _online_softmax_step(h, hd, s_ref[...], v_ref[0, :, hs], m_ref, l_ref, acc_ref)

    @pl.when(kt == qt)
    def _():
        _attn_finish(o_ref, l_ref, acc_ref, nh, hd)


def _attention(kind, big, bigm, nh, hd, tile, extra):
    b, L, _ = big.shape
    w = nh * hd
    q_blk, k_blk, v_blk = 0, 1, 2
    t = _tile(L, tile)
    nq = L // t
    q_spec = pl.BlockSpec((1, t, w), lambda bi, qi, ki: (bi, qi, q_blk))
    kv = lambda blk: pl.BlockSpec((1, t, w), lambda bi, qi, ki: (bi, jnp.minimum(ki, qi), blk))
    kvm = lambda blk: pl.BlockSpec((META_PAD, w), lambda bi, qi, ki: (0, blk))
    scratch = [
        pltpu.VMEM((nh, t, LANES), F32),
        pltpu.VMEM((nh, t, LANES), F32),
        pltpu.VMEM((t, w), F32),
    ]
    if kind == "fox":
        dt, dtm, n_meta = extra
        body = functools.partial(_fox_kernel, nh=nh, hd=hd, n_meta=n_meta)
        in_specs = [q_spec, kv(k_blk), kv(v_blk), kvm(k_blk), kvm(v_blk),
                    pl.BlockSpec((1, nh, t), lambda bi, qi, ki: (bi, 0, jnp.minimum(ki, qi))),
                    pl.BlockSpec((1, nh, META_PAD), lambda bi, qi, ki: (bi, 0, 0))]
        args = [big, big, big, bigm, bigm, dt, dtm]
    else:
        mask, band = extra
        body = functools.partial(_dsa_kernel, nh=nh, hd=hd)
        in_specs = [q_spec, kv(k_blk), kv(v_blk), kvm(k_blk), kvm(v_blk),
                    pl.BlockSpec((1, t, t), lambda bi, qi, ki: (bi, qi, jnp.minimum(ki, qi))),
                    pl.BlockSpec((1, t, META_PAD), lambda bi, qi, ki: (bi, qi, L // META_PAD)),
                    pl.BlockSpec((3, nh, LANES, LANES), lambda bi, qi, ki: (0, 0, 0, 0))]
        args = [big, big, big, bigm, bigm, mask, mask, band]
        scratch.append(pltpu.VMEM((t, t), F32))
    return pl.pallas_call(
        body,
        grid=(b, nq, nq),
        in_specs=in_specs,
        out_specs=pl.BlockSpec((1, t, w), lambda bi, qi, ki: (bi, qi, 0)),
        out_shape=jax.ShapeDtypeStruct((b, L, w), BF16),
        scratch_shapes=scratch,
        compiler_params=_cparams("arbitrary", "arbitrary", "arbitrary"),
        name=kind + "_attention",
    )(*args)


def _sortable(x):
    bits = pltpu.bitcast(x, jnp.int32)
    return bits ^ ((bits >> 31) & 0x7FFFFFFF)


def _index_kernel(qi_ref, w_ref, kie_ref, kio_ref, kiem_ref, kiom_ref, mask_ref, wb_ref, s_ref, sm_ref,
                  *, tc, rb, w_lane0, n_meta, k_top):
    qt = pl.program_id(1)
    tq = qi_ref.shape[1]
    L = kie_ref.shape[1]
    n_ct = (qt + 1) * tq // tc
    npair = H_IDX // 2
    wscale = (H_IDX ** -0.5) * (D_IDX ** -0.5)

    wv = w_ref[0]
    for h in range(H_IDX):
        col = wv[:, w_lane0 + h:w_lane0 + h + 1] * wscale
        wb_ref[h] = jnp.broadcast_to(col, (tq, LANES))

    def score_tile(ke, ko):
        n = ke.shape[0]
        parts = []
        for c in range(n // LANES):
            kec, koc = ke[c * LANES:(c + 1) * LANES], ko[c * LANES:(c + 1) * LANES]
            acc = jnp.zeros((tq, LANES), F32)
            for j in range(npair):
                lhs = qi_ref[0, :, j * LANES:(j + 1) * LANES]
                acc = acc + wb_ref[2 * j] * jnp.maximum(_dot_nt(lhs, kec), 0.0)
                acc = acc + wb_ref[2 * j + 1] * jnp.maximum(_dot_nt(lhs, koc), 0.0)
            parts.append(acc)
        return parts[0] if len(parts) == 1 else jnp.concatenate(parts, axis=1)

    colm = lax.broadcasted_iota(jnp.int32, (tq, META_PAD), 1)
    sm = jnp.where(colm < n_meta, score_tile(kiem_ref[...], kiom_ref[...]), -jnp.inf)
    sm_ref[...] = _sortable(sm)

    qchunk = (qt * tq + lax.broadcasted_iota(jnp.int32, (tq, tc), 0)) // CHUNK

    def tile_body(c, carry):
        start = pl.multiple_of(c * tc, tc)
        sc = score_tile(kie_ref[0, pl.ds(start, tc), :], kio_ref[0, pl.ds(start, tc), :])
        kchunk = (start + lax.broadcasted_iota(jnp.int32, (tq, tc), 1)) // CHUNK
        sc = jnp.where(kchunk <= qchunk, sc, -jnp.inf)
        s_ref[c] = _sortable(sc)
        return carry

    lax.fori_loop(0, n_ct, tile_body, 0)

    for r in range(tq // rb):
        rows = slice(r * rb, (r + 1) * rb)

        def count_ge(trial):
            tb = jnp.broadcast_to(trial, (rb, LANES))
            acc = (sm_ref[rows, :] >= tb).astype(jnp.int32)

            def cbody(c, acc):
                tile = s_ref[c, rows, :]
                for s in range(tc // LANES):
                    acc = acc + (tile[:, s * LANES:(s + 1) * LANES] >= tb).astype(jnp.int32)
                return acc

            acc = lax.fori_loop(0, n_ct, cbody, acc)
            return jnp.sum(acc, axis=1, keepdims=True)

        def bit_body(it, ans_u):
            trial_u = ans_u | (jnp.int32(1) << (31 - it))
            cnt = count_ge(trial_u ^ INT_MIN)
            return jnp.where(cnt >= k_top, trial_u, ans_u)

        ans_u = lax.fori_loop(0, 32, bit_body, jnp.zeros((rb, 1), jnp.int32))
        thr = jnp.maximum(ans_u ^ INT_MIN, KEY_NEG_INF + 1)
        thr_b = jnp.broadcast_to(thr, (rb, LANES))

        mask_ref[0, rows, L:] = jnp.where(sm_ref[rows, :] >= thr_b, 0.0, NEG).astype(mask_ref.dtype)
        for c in range(L // tc):
            cols = slice(c * tc, (c + 1) * tc)

            @pl.when(c < n_ct)
            def _():
                sel = s_ref[c, rows, :] >= jnp.tile(thr_b, (1, tc // LANES))
                mask_ref[0, rows, cols] = jnp.where(sel, 0.0, NEG).astype(mask_ref.dtype)

            @pl.when(c >= n_ct)
            def _():
                mask_ref[0, rows, cols] = jnp.full((rb, tc), NEG, mask_ref.dtype)


def _indexer_mask(qi, slab, kie, kio, kiem, kiom, w_lane0, n_meta, k_top, tq=256, tc=256, rb=64):
    b, L, _ = qi.shape
    tq = _tile(L, tq)
    tc = _tile(tq, tc)
    assert tq % rb == 0
    wq = H_IDX * D_IDX
    return pl.pallas_call(
        functools.partial(_index_kernel, tc=tc, rb=rb, w_lane0=w_lane0, n_meta=n_meta, k_top=k_top),
        grid=(b, L // tq),
        in_specs=[
            pl.BlockSpec((1, tq, wq), lambda bi, qi: (bi, qi, 0)),
            pl.BlockSpec((1, tq, LANES), lambda bi, qi: (bi, qi, 0)),
            pl.BlockSpec((1, L, LANES), lambda bi, qi: (bi, 0, 0)),
            pl.BlockSpec((1, L, LANES), lambda bi, qi: (bi, 0, 0)),
            pl.BlockSpec((META_PAD, LANES), lambda bi, qi: (0, 0)),
            pl.BlockSpec((META_PAD, LANES), lambda bi, qi: (0, 0)),
        ],
        out_specs=pl.BlockSpec((1, tq, L + META_PAD), lambda bi, qi: (bi, qi, 0)),
        out_shape=jax.ShapeDtypeStruct((b, L, L + META_PAD), BF16),
        scratch_shapes=[
            pltpu.VMEM((H_IDX, tq, LANES), F32),
            pltpu.VMEM((L // tc, tq, tc), jnp.int32),
            pltpu.VMEM((tq, META_PAD), jnp.int32),
        ],
        compiler_params=_cparams("arbitrary", "arbitrary"),
        name="indexer_topk_mask",
    )(qi, slab, kie, kio, kiem, kiom)


def _t5_bucket(rel, n_buckets):
    half = n_buckets // 2
    max_exact = half // 2
    ret = jnp.where(rel > 0, half, 0)
    n = jnp.abs(rel)
    nf = jnp.maximum(n, 1).astype(jnp.float32)
    large = max_exact + (jnp.log(nf / max_exact) / math.log(MAX_DISTANCE / max_exact)
                         * (half - max_exact)).astype(jnp.int32)
    large = jnp.minimum(large, half - 1)
    return ret + jnp.where(n < max_exact, n, large)


def _band_kernel(bucket_ref, far_ref, rb_ref, o_ref, *, n_buckets, nh):
    bucket = bucket_ref[0]
    far = far_ref[0]
    for h in range(nh):
        acc = jnp.zeros(bucket.shape, F32)
        for bk in range(n_buckets):
            acc = jnp.where(bucket == bk, rb_ref[bk, h], acc)
        o_ref[0, h] = acc - rb_ref[far, h]


def _band_tables(rel_bias, n_meta):
    n_buckets, nh = rel_bias.shape
    i = jnp.arange(LANES, dtype=jnp.int32)[:, None]
    j = jnp.arange(LANES, dtype=jnp.int32)[None, :]
    rel = jnp.stack([j - i, j - i - LANES, j - n_meta - i])
    bucket = _t5_bucket(rel, n_buckets).astype(jnp.int32)
    far = _t5_bucket(jnp.full((1,), -MAX_DISTANCE, jnp.int32), n_buckets).astype(jnp.int32)
    return pl.pallas_call(
        functools.partial(_band_kernel, n_buckets=n_buckets, nh=nh),
        grid=(3,),
        in_specs=[
            pl.BlockSpec((1, LANES, LANES), lambda t: (t, 0, 0)),
            pl.BlockSpec(memory_space=pltpu.SMEM),
            pl.BlockSpec(memory_space=pltpu.SMEM),
        ],
        out_specs=pl.BlockSpec((1, nh, LANES, LANES), lambda t: (t, 0, 0, 0)),
        out_shape=jax.ShapeDtypeStruct((3, nh, LANES, LANES), F32),
        compiler_params=_cparams("arbitrary"),
        name="rel_bias_band",
    )(bucket, far, rel_bias.astype(F32))


def kernel(x, meta_tokens, attn_norm_g, w_in, forget_bias, rel_bias, w_branch_fox, w_branch_dsa,
           w_out, mlp_norm_g, w_up, w_down, final_norm_g):
    b, L, d = x.shape
    n_meta = meta_tokens.shape[0]
    h_fox = forget_bias.shape[1]
    h_dsa = rel_bias.shape[1]
    w_fox, w_dsa = w_branch_fox.shape[1], w_branch_dsa.shape[1]
    hd = w_fox // h_fox
    w_idx = H_IDX * D_IDX
    assert attn_norm_g.shape[0] == 1, "single-layer block"
    assert w_dsa // h_dsa == hd and hd % LANES == 0
    assert n_meta <= META_PAD and L % CHUNK == 0 and L % META_PAD == 0
    assert MAX_DISTANCE <= LANES and CHUNK <= LANES
    k_top = min(TOPK_MAX, L // 4)

    wi = w_in[0]
    o = 0
    cols = {}
    for name, width in (("qa", w_fox), ("ka", w_fox), ("va", w_fox), ("fa", h_fox), ("qb", w_dsa),
                        ("kb", w_dsa), ("vb", w_dsa), ("qi", w_idx), ("ki", D_IDX), ("wi", H_IDX),
                        ("gl", 2 * d)):
        cols[name] = wi[:, o:o + width]
        o += width
    assert o == wi.shape[1]
    w_fox3 = jnp.concatenate([cols[n] for n in ("qa", "ka", "va")], axis=1).astype(BF16)
    w_dsa3 = jnp.concatenate([cols[n] for n in ("qb", "kb", "vb")], axis=1).astype(BF16)
    w_qi = cols["qi"].astype(BF16)
    w_gate = cols["gl"].astype(BF16)
    n_small = D_IDX + H_IDX + h_fox
    assert n_small <= LANES
    w_small = jnp.concatenate([cols["ki"], cols["wi"], cols["fa"],
                               jnp.zeros((d, LANES - n_small), wi.dtype)], axis=1).astype(BF16)
    wi_lane0, fa_lane0 = D_IDX, D_IDX + H_IDX

    x2 = x.reshape(b * L, d)
    meta = jnp.zeros((META_PAD, d), x.dtype).at[:n_meta].set(meta_tokens.astype(x.dtype))
    u = _rmsnorm(x2, attn_norm_g[0], BF16)
    um = _rmsnorm(meta, attn_norm_g[0], BF16)
    fox3 = _mm(u, w_fox3, BF16).reshape(b, L, -1)
    fox3m = _mm(um, w_fox3, BF16)
    dsa3 = _mm(u, w_dsa3, BF16).reshape(b, L, -1)
    dsa3m = _mm(um, w_dsa3, BF16)
    qi = _mm(u, w_qi, BF16).reshape(b, L, -1)
    gate = _mm(u, w_gate, F32, act="sigmoid")
    slab = _mm(u, w_small, F32, tn=LANES).reshape(b, L, LANES)
    slabm = _mm(um, w_small, F32, tn=LANES)

    bias_row = jnp.zeros((1, LANES), F32).at[0, fa_lane0:fa_lane0 + h_fox].set(forget_bias[0].astype(F32))
    dt, dtm = _decay(slab, slabm, bias_row, fa_lane0, h_fox, n_meta)
    o_fox = _attention("fox", fox3, fox3m, h_fox, hd, 512, (dt, dtm, n_meta))

    zpad = jnp.zeros(slab.shape[:-1] + (LANES - D_IDX,), BF16)
    ki = slab[..., :D_IDX].astype(BF16)
    kie, kio = jnp.concatenate([ki, zpad], axis=-1), jnp.concatenate([zpad, ki], axis=-1)
    kim = slabm[:, :D_IDX].astype(BF16)
    kiem = jnp.concatenate([kim, zpad[0, :META_PAD]], axis=-1)
    kiom = jnp.concatenate([zpad[0, :META_PAD], kim], axis=-1)
    mask = _indexer_mask(qi, slab, kie, kio, kiem, kiom, wi_lane0, n_meta, k_top)
    band = _band_tables(rel_bias, n_meta)
    o_dsa = _attention("dsa", dsa3, dsa3m, h_dsa, hd, 512, (mask, band))

    mixed = _branch_merge(o_fox.reshape(b * L, w_fox), o_dsa.reshape(b * L, w_dsa),
                          w_branch_fox[0].astype(BF16), w_branch_dsa[0].astype(BF16), gate, BF16)
    h2 = _mm(mixed, w_out[0].astype(BF16), F32, res=x2)

    u2 = _rmsnorm(h2, mlp_norm_g[0], BF16)
    a = _mm(u2, w_up[0].astype(BF16), BF16, act="relu2")
    h3 = _mm_acc(a, w_down[0].astype(BF16), h2, F32)
    return _rmsnorm(h3, final_norm_g, x.dtype).reshape(b, L, d)
```

```python
import functools
import math

import jax
import jax.numpy as jnp
from jax import lax
from jax.experimental import pallas as pl
from jax.experimental.pallas import tpu as pltpu

F32 = jnp.float32
BF16 = jnp.bfloat16

CHUNK = 64
H_IDX = 32
D_IDX = 64
TOPK_MAX = 256
MAX_DISTANCE = 128
RMS_EPS = 1e-6

LANES = 128
META_PAD = 128
NEG = -1e30
LOG2E = math.log2(math.e)
INT_MIN = -(2 ** 31)
KEY_NEG_INF = -2139095041
VMEM_LIMIT = 56 * 1024 * 1024


def _tile(dim, pref):
    if dim <= pref:
        return dim
    for t in range(pref - pref % LANES, 0, -LANES):
        if dim % t == 0:
            return t
    raise ValueError((dim, pref))


def _cparams(*sem):
    return pltpu.CompilerParams(dimension_semantics=sem, vmem_limit_bytes=VMEM_LIMIT)


def _rmsnorm_kernel(x_ref, g_ref, o_ref):
    x = x_ref[...].astype(F32)
    ms = jnp.mean(x * x, axis=-1, keepdims=True)
    y = x * lax.rsqrt(ms + RMS_EPS)
    o_ref[...] = (y * g_ref[...]).astype(o_ref.dtype)


def _rmsnorm(x, g, out_dtype):
    m, d = x.shape
    tr = _tile(m, 256)
    return pl.pallas_call(
        _rmsnorm_kernel,
        grid=(m // tr,),
        in_specs=[pl.BlockSpec((tr, d), lambda i: (i, 0)), pl.BlockSpec((1, d), lambda i: (0, 0))],
        out_specs=pl.BlockSpec((tr, d), lambda i: (i, 0)),
        out_shape=jax.ShapeDtypeStruct((m, d), out_dtype),
        compiler_params=_cparams("arbitrary"),
        name="rmsnorm",
    )(x, g.reshape(1, d).astype(F32))


def _act(acc, act):
    if act == "sigmoid":
        return jax.nn.sigmoid(acc)
    if act == "relu2":
        r = jnp.maximum(acc, 0.0)
        return r * r
    return acc


def _mm_kernel(a_ref, b_ref, o_ref, *, act):
    acc = jnp.dot(a_ref[...], b_ref[...], preferred_element_type=F32)
    o_ref[...] = _act(acc, act).astype(o_ref.dtype)


def _mm_res_kernel(a_ref, b_ref, r_ref, o_ref):
    acc = jnp.dot(a_ref[...], b_ref[...], preferred_element_type=F32)
    o_ref[...] = (r_ref[...] + acc).astype(o_ref.dtype)


def _mm(a, b, out_dtype, act=None, res=None, tm=1024, tn=512):
    m, k = a.shape
    _, n = b.shape
    tm, tn = _tile(m, tm), _tile(n, tn)
    in_specs = [pl.BlockSpec((tm, k), lambda i, j: (i, 0)), pl.BlockSpec((k, tn), lambda i, j: (0, j))]
    args = [a, b]
    if res is None:
        body = functools.partial(_mm_kernel, act=act)
    else:
        body = _mm_res_kernel
        in_specs.append(pl.BlockSpec((tm, tn), lambda i, j: (i, j)))
        args.append(res)
    return pl.pallas_call(
        body,
        grid=(m // tm, n // tn),
        in_specs=in_specs,
        out_specs=pl.BlockSpec((tm, tn), lambda i, j: (i, j)),
        out_shape=jax.ShapeDtypeStruct((m, n), out_dtype),
        compiler_params=_cparams("arbitrary", "arbitrary"),
        name="matmul",
    )(*args)


def _mm_acc_kernel(a_ref, b_ref, r_ref, o_ref, acc_ref):
    kk = pl.program_id(2)

    @pl.when(kk == 0)
    def _():
        acc_ref[...] = jnp.zeros_like(acc_ref)

    acc_ref[...] += jnp.dot(a_ref[...], b_ref[...], preferred_element_type=F32)

    @pl.when(kk == pl.num_programs(2) - 1)
    def _():
        o_ref[...] = (r_ref[...] + acc_ref[...]).astype(o_ref.dtype)


def _mm_acc(a, b, res, out_dtype, tm=1024, tn=1024, tk=2048):
    m, k = a.shape
    _, n = b.shape
    tm, tn, tk = _tile(m, tm), _tile(n, tn), _tile(k, tk)
    return pl.pallas_call(
        _mm_acc_kernel,
        grid=(m // tm, n // tn, k // tk),
        in_specs=[
            pl.BlockSpec((tm, tk), lambda i, j, kk: (i, kk)),
            pl.BlockSpec((tk, tn), lambda i, j, kk: (kk, j)),
            pl.BlockSpec((tm, tn), lambda i, j, kk: (i, j)),
        ],
        out_specs=pl.BlockSpec((tm, tn), lambda i, j, kk: (i, j)),
        out_shape=jax.ShapeDtypeStruct((m, n), out_dtype),
        scratch_shapes=[pltpu.VMEM((tm, tn), F32)],
        compiler_params=_cparams("arbitrary", "arbitrary", "arbitrary"),
        name="matmul_acc",
    )(a, b, res)


def _branch_kernel(of_ref, od_ref, wf_ref, wd_ref, g0_ref, g1_ref, o_ref):
    yf = jnp.dot(of_ref[...], wf_ref[...], preferred_element_type=F32)
    yd = jnp.dot(od_ref[...], wd_ref[...], preferred_element_type=F32)
    o_ref[...] = (g0_ref[...] * yf + g1_ref[...] * yd).astype(o_ref.dtype)


def _branch_merge(o_fox, o_dsa, wf, wd, gate, out_dtype, tm=512, tn=1024):
    m, kf = o_fox.shape
    _, kd = o_dsa.shape
    n = wf.shape[1]
    tm, tn = _tile(m, tm), _tile(n, tn)
    nj = n // tn
    return pl.pallas_call(
        _branch_kernel,
        grid=(m // tm, nj),
        in_specs=[
            pl.BlockSpec((tm, kf), lambda i, j: (i, 0)),
            pl.BlockSpec((tm, kd), lambda i, j: (i, 0)),
            pl.BlockSpec((kf, tn), lambda i, j: (0, j)),
            pl.BlockSpec((kd, tn), lambda i, j: (0, j)),
            pl.BlockSpec((tm, tn), lambda i, j: (i, j)),
            pl.BlockSpec((tm, tn), lambda i, j: (i, j + nj)),
        ],
        out_specs=pl.BlockSpec((tm, tn), lambda i, j: (i, j)),
        out_shape=jax.ShapeDtypeStruct((m, n), out_dtype),
        compiler_params=_cparams("arbitrary", "arbitrary"),
        name="branch_merge",
    )(o_fox, o_dsa, wf, wd, gate, gate)


def _split3(x):
    hi = x.astype(BF16)
    r1 = x - hi.astype(F32)
    mid = r1.astype(BF16)
    lo = (r1 - mid.astype(F32)).astype(BF16)
    return hi, mid, lo


def _cumsum_lanes(xt, upper):
    out = None
    for part in _split3(xt):
        d = jnp.dot(part, upper, preferred_element_type=F32)
        out = d if out is None else out + d
    return out


def _decay_kernel(slab_ref, slabm_ref, bias_ref, dt_ref, dtm_ref, *, blk, lane0, nh, n_meta):
    L = slab_ref.shape[1]
    bias = bias_ref[...]

    def upper(n):
        r = lax.broadcasted_iota(jnp.int32, (n, n), 0)
        c = lax.broadcasted_iota(jnp.int32, (n, n), 1)
        return (r <= c).astype(BF16)

    xm = jax.nn.log_sigmoid(slabm_ref[...] + bias)
    dm = _cumsum_lanes(xm.T, upper(META_PAD))
    dtm_ref[0] = dm[lane0:lane0 + nh, :] * -LOG2E
    carry = dm[:, n_meta - 1:n_meta]
    up = upper(blk)
    for i in range(L // blk):
        x = jax.nn.log_sigmoid(slab_ref[0, i * blk:(i + 1) * blk, :] + bias)
        d = _cumsum_lanes(x.T, up) + carry
        dt_ref[0, :, i * blk:(i + 1) * blk] = d[lane0:lane0 + nh, :] * -LOG2E
        carry = d[:, blk - 1:blk]


def _decay(slab, slabm, bias_row, lane0, nh, n_meta):
    b, L, _ = slab.shape
    blk = _tile(L, 256)
    return pl.pallas_call(
        functools.partial(_decay_kernel, blk=blk, lane0=lane0, nh=nh, n_meta=n_meta),
        grid=(b,),
        in_specs=[
            pl.BlockSpec((1, L, LANES), lambda i: (i, 0, 0)),
            pl.BlockSpec((META_PAD, LANES), lambda i: (0, 0)),
            pl.BlockSpec((1, LANES), lambda i: (0, 0)),
        ],
        out_specs=[
            pl.BlockSpec((1, nh, L), lambda i: (i, 0, 0)),
            pl.BlockSpec((1, nh, META_PAD), lambda i: (i, 0, 0)),
        ],
        out_shape=[
            jax.ShapeDtypeStruct((b, nh, L), F32),
            jax.ShapeDtypeStruct((b, nh, META_PAD), F32),
        ],
        compiler_params=_cparams("arbitrary"),
        name="forget_cumsum",
    )(slab, slabm, bias_row)


def _dot_nt(a, b):
    return lax.dot_general(a, b, (((1,), (1,)), ((), ())), preferred_element_type=F32)


def _online_softmax_step(h, hd, s2, v, m_ref, l_ref, acc_ref):
    ts = s2.shape[1]
    m_prev = m_ref[h]
    m_new = jnp.maximum(m_prev, jnp.max(s2, axis=1, keepdims=True))
    alpha = jnp.exp2(m_prev - m_new)
    p = jnp.exp2(s2 - jnp.tile(m_new, (1, ts // LANES))).astype(BF16)
    v_ones = jnp.concatenate([v, jnp.ones((ts, LANES), v.dtype)], axis=1)
    pv = jnp.dot(p, v_ones, preferred_element_type=F32)
    hs = slice(h * hd, (h + 1) * hd)
    acc_ref[:, hs] = acc_ref[:, hs] * jnp.tile(alpha, (1, hd // LANES)) + pv[:, :hd]
    l_ref[h] = alpha * l_ref[h] + pv[:, hd:]
    m_ref[h] = m_new


def _attn_init(m_ref, l_ref, acc_ref):
    m_ref[...] = jnp.full_like(m_ref, NEG)
    l_ref[...] = jnp.zeros_like(l_ref)
    acc_ref[...] = jnp.zeros_like(acc_ref)


def _attn_finish(o_ref, l_ref, acc_ref, nh, hd):
    for h in range(nh):
        hs = slice(h * hd, (h + 1) * hd)
        o_ref[0, :, hs] = (acc_ref[:, hs] / jnp.tile(l_ref[h], (1, hd // LANES))).astype(o_ref.dtype)


def _fox_kernel(q_ref, k_ref, v_ref, km_ref, vm_ref, dt_ref, dtm_ref, o_ref, m_ref, l_ref, acc_ref,
                *, nh, hd, n_meta):
    qt, kt = pl.program_id(1), pl.program_id(2)
    tq, ts = q_ref.shape[1], k_ref.shape[1]
    scale = hd ** -0.5 * LOG2E

    @pl.when(kt == 0)
    def _():
        _attn_init(m_ref, l_ref, acc_ref)

    def frames(diag):
        if diag:
            row = lax.broadcasted_iota(jnp.int32, (tq, ts), 0)
            col = lax.broadcasted_iota(jnp.int32, (tq, ts), 1)
            causal = col <= row
            colm = lax.broadcasted_iota(jnp.int32, (tq, META_PAD), 1)
            pad = jnp.where(colm < n_meta, 0.0, NEG).astype(F32)
        for h in range(nh):
            hs = slice(h * hd, (h + 1) * hd)
            q = q_ref[0, :, hs]
            s = _dot_nt(q, k_ref[0, :, hs]) * scale + dt_ref[0, h:h + 1, :]
            v = v_ref[0, :, hs]
            if diag:
                sm = _dot_nt(q, km_ref[:, hs]) * scale + dtm_ref[0, h:h + 1, :] + pad
                s = jnp.concatenate([jnp.where(causal, s, NEG), sm], axis=1)
                v = jnp.concatenate([v, vm_ref[:, hs]], axis=0)
            _online_softmax_step(h, hd, s, v, m_ref, l_ref, acc_ref)

    @pl.when(kt < qt)
    def _():
        frames(False)

    @pl.when(kt == qt)
    def _():
        frames(True)
        _attn_finish(o_ref, l_ref, acc_ref, nh, hd)


def _add_blocks(s, adds):
    rows = []
    for ib in range(s.shape[0] // LANES):
        r = s[ib * LANES:(ib + 1) * LANES]
        if any(i == ib for i, _ in adds):
            r = jnp.concatenate(
                [r[:, jb * LANES:(jb + 1) * LANES] + adds[(ib, jb)] if (ib, jb) in adds
                 else r[:, jb * LANES:(jb + 1) * LANES] for jb in range(s.shape[1] // LANES)], axis=1)
        rows.append(r)
    return jnp.concatenate(rows, axis=0)


def _dsa_kernel(q_ref, k_ref, v_ref, km_ref, vm_ref, mask_ref, maskm_ref, band_ref, o_ref,
                m_ref, l_ref, acc_ref, *, nh, hd):
    qt, kt = pl.program_id(1), pl.program_id(2)
    tq, ts = q_ref.shape[1], k_ref.shape[1]
    nb = tq // LANES
    scale = hd ** -0.5 * LOG2E

    @pl.when(kt == 0)
    def _():
        _attn_init(m_ref, l_ref, acc_ref)

    def frames(kind):
        maskf = mask_ref[0].astype(F32)
        if kind == "diag":
            maskm = maskm_ref[0].astype(F32)
            first = (qt == 0).astype(F32)
        for h in range(nh):
            hs = slice(h * hd, (h + 1) * hd)
            q = q_ref[0, :, hs]
            s = _dot_nt(q, k_ref[0, :, hs]) * scale + maskf
            v = v_ref[0, :, hs]
            if kind == "near":
                s = _add_blocks(s, {(0, ts // LANES - 1): band_ref[1, h]})
            elif kind == "diag":
                sm = _dot_nt(q, km_ref[:, hs]) * scale + maskm
                s = jnp.concatenate([s, sm], axis=1)
                adds = {(0, ts // LANES): band_ref[2, h] * first}
                for ib in range(nb):
                    adds[(ib, ib)] = band_ref[0, h]
                    if ib > 0:
                        adds[(ib, ib - 1)] = band_ref[1, h]
                s = _add_blocks(s, adds)
                v = jnp.concatenate([v, vm_ref[:, hs]], axis=0)
            _online_softmax_step(h, hd, s, v, m_ref, l_ref, acc_ref)

    @pl.when(kt < qt - 1)
    def _():
        frames("far")

    @pl.when(kt == qt - 1)
    def _():
        frames("near")

    @pl.when(kt == qt)
    def _():
        frames("diag")
        _attn_finish(o_ref, l_ref, acc_ref, nh, hd)


def _attention(kind, big, bigm, nh, hd, tile, extra):
    b, L, _ = big.shape
    w = nh * hd
    q_blk, k_blk, v_blk = 0, 1, 2
    t = _tile(L, tile)
    nq = L // t
    q_spec = pl.BlockSpec((1, t, w), lambda bi, qi, ki: (bi, qi, q_blk))
    kv = lambda blk: pl.BlockSpec((1, t, w), lambda bi, qi, ki: (bi, jnp.minimum(ki, qi), blk))
    kvm = lambda blk: pl.BlockSpec((META_PAD, w), lambda bi, qi, ki: (0, blk))
    scratch = [
        pltpu.VMEM((nh, t, LANES), F32),
        pltpu.VMEM((nh, t, LANES), F32),
        pltpu.VMEM((t, w), F32),
    ]
    if kind == "fox":
        dt, dtm, n_meta = extra
        body = functools.partial(_fox_kernel, nh=nh, hd=hd, n_meta=n_meta)
        in_specs = [q_spec, kv(k_blk), kv(v_blk), kvm(k_blk), kvm(v_blk),
                    pl.BlockSpec((1, nh, t), lambda bi, qi, ki: (bi, 0, jnp.minimum(ki, qi))),
                    pl.BlockSpec((1, nh, META_PAD), lambda bi, qi, ki: (bi, 0, 0))]
        args = [big, big, big, bigm, bigm, dt, dtm]
    else:
        mask, band = extra
        body = functools.partial(_dsa_kernel, nh=nh, hd=hd)
        in_specs = [q_spec, kv(k_blk), kv(v_blk), kvm(k_blk), kvm(v_blk),
                    pl.BlockSpec((1, t, t), lambda bi, qi, ki: (bi, qi, jnp.minimum(ki, qi))),
                    pl.BlockSpec((1, t, META_PAD), lambda bi, qi, ki: (bi, qi, L // META_PAD)),
                    pl.BlockSpec((3, nh, LANES, LANES), lambda bi, qi, ki: (0, 0, 0, 0))]
        args = [big, big, big, bigm, bigm, mask, mask, band]
    return pl.pallas_call(
        body,
        grid=(b, nq, nq),
        in_specs=in_specs,
        out_specs=pl.BlockSpec((1, t, w), lambda bi, qi, ki: (bi, qi, 0)),
        out_shape=jax.ShapeDtypeStruct((b, L, w), BF16),
        scratch_shapes=scratch,
        compiler_params=_cparams("arbitrary", "arbitrary", "arbitrary"),
        name=kind + "_attention",
    )(*args)


def _sortable(x):
    bits = pltpu.bitcast(x, jnp.int32)
    return bits ^ ((bits >> 31) & 0x7FFFFFFF)


def _index_kernel(qi_ref, w_ref, kie_ref, kio_ref, kiem_ref, kiom_ref, mask_ref, wb_ref, st_ref, smt_ref,
                  *, tc, w_lane0, n_meta, k_top):
    qt = pl.program_id(1)
    tq = qi_ref.shape[1]
    L = kie_ref.shape[1]
    n_ct = (qt + 1) * tq // tc
    npair = H_IDX // 2
    wscale = (H_IDX ** -0.5) * (D_IDX ** -0.5)

    wv = w_ref[0]
    for h in range(H_IDX):
        col = wv[:, w_lane0 + h:w_lane0 + h + 1] * wscale
        wb_ref[h] = jnp.broadcast_to(col, (tq, LANES))

    def score_tile(ke, ko):
        n = ke.shape[0]
        accs = [jnp.zeros((tq, LANES), F32) for _ in range(n // LANES)]
        for j in range(npair):
            lhs = qi_ref[0, :, j * LANES:(j + 1) * LANES]
            de = jnp.maximum(_dot_nt(lhs, ke), 0.0)
            do = jnp.maximum(_dot_nt(lhs, ko), 0.0)
            for c in range(n // LANES):
                cs = slice(c * LANES, (c + 1) * LANES)
                accs[c] = accs[c] + wb_ref[2 * j] * de[:, cs] + wb_ref[2 * j + 1] * do[:, cs]
        return accs[0] if len(accs) == 1 else jnp.concatenate(accs, axis=1)

    colm = lax.broadcasted_iota(jnp.int32, (tq, META_PAD), 1)
    sm = jnp.where(colm < n_meta, score_tile(kiem_ref[...], kiom_ref[...]), -jnp.inf)
    smt_ref[...] = _sortable(sm.T)

    qchunk = (qt * tq + lax.broadcasted_iota(jnp.int32, (tq, tc), 0)) // CHUNK

    def tile_body(c, carry):
        start = pl.multiple_of(c * tc, tc)
        sc = score_tile(kie_ref[0, pl.ds(start, tc), :], kio_ref[0, pl.ds(start, tc), :])
        kchunk = (start + lax.broadcasted_iota(jnp.int32, (tq, tc), 1)) // CHUNK
        sc = jnp.where(kchunk <= qchunk, sc, -jnp.inf)
        st_ref[c] = _sortable(sc.T)
        return carry

    lax.fori_loop(0, n_ct, tile_body, 0)

    sub = 8
    nacc = 4

    def count_tile(tile, thr8, accs):
        for r in range(tile.shape[0] // sub):
            accs[r % nacc] = accs[r % nacc] + (tile[r * sub:(r + 1) * sub, :] >= thr8).astype(jnp.int32)
        return accs

    def count_ge(thr8):
        accs = count_tile(smt_ref[...], thr8, [jnp.zeros((sub, tq), jnp.int32) for _ in range(nacc)])
        accs = lax.fori_loop(0, n_ct, lambda c, a: tuple(count_tile(st_ref[c], thr8, list(a))), tuple(accs))
        total = functools.reduce(lambda x, y: x + y, accs)
        return jnp.broadcast_to(jnp.sum(total, axis=0, keepdims=True), (sub, tq))

    def bit_body(it, ans_u):
        trial_u = ans_u | (jnp.int32(1) << (31 - it))
        cnt = count_ge(trial_u ^ INT_MIN)
        return jnp.where(cnt >= k_top, trial_u, ans_u)

    ans_u = lax.fori_loop(0, 32, bit_body, jnp.zeros((sub, tq), jnp.int32))
    thr = jnp.maximum(ans_u ^ INT_MIN, KEY_NEG_INF + 1)[:1]

    def mask_of(keys_t):
        return jnp.where(keys_t >= thr, 0.0, NEG).astype(F32).T.astype(mask_ref.dtype)

    mask_ref[0, :, L:] = mask_of(smt_ref[...])
    for c in range(L // tc):
        cols = slice(c * tc, (c + 1) * tc)

        @pl.when(c < n_ct)
        def _():
            mask_ref[0, :, cols] = mask_of(st_ref[c])

        @pl.when(c >= n_ct)
        def _():
            mask_ref[0, :, cols] = jnp.full((tq, tc), NEG, mask_ref.dtype)


def _indexer_mask(qi, slab, kie, kio, kiem, kiom, w_lane0, n_meta, k_top, tq=256, tc=256):
    b, L, _ = qi.shape
    tq = _tile(L, tq)
    tc = _tile(tq, tc)
    wq = H_IDX * D_IDX
    return pl.pallas_call(
        functools.partial(_index_kernel, tc=tc, w_lane0=w_lane0, n_meta=n_meta, k_top=k_top),
        grid=(b, L // tq),
        in_specs=[
            pl.BlockSpec((1, tq, wq), lambda bi, qi: (bi, qi, 0)),
            pl.BlockSpec((1, tq, LANES), lambda bi, qi: (bi, qi, 0)),
            pl.BlockSpec((1, L, LANES), lambda bi, qi: (bi, 0, 0)),
            pl.BlockSpec((1, L, LANES), lambda bi, qi: (bi, 0, 0)),
            pl.BlockSpec((META_PAD, LANES), lambda bi, qi: (0, 0)),
            pl.BlockSpec((META_PAD, LANES), lambda bi, qi: (0, 0)),
        ],
        out_specs=pl.BlockSpec((1, tq, L + META_PAD), lambda bi, qi: (bi, qi, 0)),
        out_shape=jax.ShapeDtypeStruct((b, L, L + META_PAD), BF16),
        scratch_shapes=[
            pltpu.VMEM((H_IDX, tq, LANES), F32),
            pltpu.VMEM((L // tc, tc, tq), jnp.int32),
            pltpu.VMEM((META_PAD, tq), jnp.int32),
        ],
        compiler_params=_cparams("arbitrary", "arbitrary"),
        name="indexer_topk_mask",
    )(qi, slab, kie, kio, kiem, kiom)


def _t5_bucket(rel, n_buckets):
    half = n_buckets // 2
    max_exact = half // 2
    ret = jnp.where(rel > 0, half, 0)
    n = jnp.abs(rel)
    nf = jnp.maximum(n, 1).astype(jnp.float32)
    large = max_exact + (jnp.log(nf / max_exact) / math.log(MAX_DISTANCE / max_exact)
                         * (half - max_exact)).astype(jnp.int32)
    large = jnp.minimum(large, half - 1)
    return ret + jnp.where(n < max_exact, n, large)


def _band_kernel(bucket_ref, far_ref, rb_ref, o_ref, *, n_buckets, nh):
    bucket = bucket_ref[0]
    far = far_ref[0]
    for h in range(nh):
        acc = jnp.zeros(bucket.shape, F32)
        for bk in range(n_buckets):
            acc = jnp.where(bucket == bk, rb_ref[bk, h], acc)
        o_ref[0, h] = (acc - rb_ref[far, h]) * LOG2E


def _band_tables(rel_bias, n_meta):
    n_buckets, nh = rel_bias.shape
    i = jnp.arange(LANES, dtype=jnp.int32)[:, None]
    j = jnp.arange(LANES, dtype=jnp.int32)[None, :]
    rel = jnp.stack([j - i, j - i - LANES, j - n_meta - i])
    bucket = _t5_bucket(rel, n_buckets).astype(jnp.int32)
    far = _t5_bucket(jnp.full((1,), -MAX_DISTANCE, jnp.int32), n_buckets).astype(jnp.int32)
    return pl.pallas_call(
        functools.partial(_band_kernel, n_buckets=n_buckets, nh=nh),
        grid=(3,),
        in_specs=[
            pl.BlockSpec((1, LANES, LANES), lambda t: (t, 0, 0)),
            pl.BlockSpec(memory_space=pltpu.SMEM),
            pl.BlockSpec(memory_space=pltpu.SMEM),
        ],
        out_specs=pl.BlockSpec((1, nh, LANES, LANES), lambda t: (t, 0, 0, 0)),
        out_shape=jax.ShapeDtypeStruct((3, nh, LANES, LANES), F32),
        compiler_params=_cparams("arbitrary"),
        name="rel_bias_band",
    )(bucket, far, rel_bias.astype(F32))


def kernel(x, meta_tokens, attn_norm_g, w_in, forget_bias, rel_bias, w_branch_fox, w_branch_dsa,
           w_out, mlp_norm_g, w_up, w_down, final_norm_g):
    b, L, d = x.shape
    n_meta = meta_tokens.shape[0]
    h_fox = forget_bias.shape[1]
    h_dsa = rel_bias.shape[1]
    w_fox, w_dsa = w_branch_fox.shape[1], w_branch_dsa.shape[1]
    hd = w_fox // h_fox
    w_idx = H_IDX * D_IDX
    assert attn_norm_g.shape[0] == 1, "single-layer block"
    assert w_dsa // h_dsa == hd and hd % LANES == 0
    assert n_meta <= META_PAD and L % CHUNK == 0 and L % META_PAD == 0
    assert MAX_DISTANCE <= LANES and CHUNK <= LANES
    k_top = min(TOPK_MAX, L // 4)

    wi = w_in[0]
    o = 0
    cols = {}
    for name, width in (("qa", w_fox), ("ka", w_fox), ("va", w_fox), ("fa", h_fox), ("qb", w_dsa),
                        ("kb", w_dsa), ("vb", w_dsa), ("qi", w_idx), ("ki", D_IDX), ("wi", H_IDX),
                        ("gl", 2 * d)):
        cols[name] = wi[:, o:o + width]
        o += width
    assert o == wi.shape[1]
    w_fox3 = jnp.concatenate([cols[n] for n in ("qa", "ka", "va")], axis=1).astype(BF16)
    w_dsa3 = jnp.concatenate([cols[n] for n in ("qb", "kb", "vb")], axis=1).astype(BF16)
    w_qi = cols["qi"].astype(BF16)
    w_gate = cols["gl"].astype(BF16)
    n_small = D_IDX + H_IDX + h_fox
    assert n_small <= LANES
    w_small = jnp.concatenate([cols["ki"], cols["wi"], cols["fa"],
                               jnp.zeros((d, LANES - n_small), wi.dtype)], axis=1).astype(BF16)
    wi_lane0, fa_lane0 = D_IDX, D_IDX + H_IDX

    x2 = x.reshape(b * L, d)
    meta = jnp.zeros((META_PAD, d), x.dtype).at[:n_meta].set(meta_tokens.astype(x.dtype))
    u = _rmsnorm(x2, attn_norm_g[0], BF16)
    um = _rmsnorm(meta, attn_norm_g[0], BF16)
    fox3 = _mm(u, w_fox3, BF16).reshape(b, L, -1)
    fox3m = _mm(um, w_fox3, BF16)
    dsa3 = _mm(u, w_dsa3, BF16).reshape(b, L, -1)
    dsa3m = _mm(um, w_dsa3, BF16)
    qi = _mm(u, w_qi, BF16).reshape(b, L, -1)
    gate = _mm(u, w_gate, F32, act="sigmoid")
    slab = _mm(u, w_small, F32, tn=LANES).reshape(b, L, LANES)
    slabm = _mm(um, w_small, F32, tn=LANES)

    bias_row = jnp.zeros((1, LANES), F32).at[0, fa_lane0:fa_lane0 + h_fox].set(forget_bias[0].astype(F32))
    dt, dtm = _decay(slab, slabm, bias_row, fa_lane0, h_fox, n_meta)
    o_fox = _attention("fox", fox3, fox3m, h_fox, hd, 512, (dt, dtm, n_meta))

    zpad = jnp.zeros(slab.shape[:-1] + (LANES - D_IDX,), BF16)
    ki = slab[..., :D_IDX].astype(BF16)
    kie, kio = jnp.concatenate([ki, zpad], axis=-1), jnp.concatenate([zpad, ki], axis=-1)
    kim = slabm[:, :D_IDX].astype(BF16)
    kiem = jnp.concatenate([kim, zpad[0, :META_PAD]], axis=-1)
    kiom = jnp.concatenate([zpad[0, :META_PAD], kim], axis=-1)
    mask = _indexer_mask(qi, slab, kie, kio, kiem, kiom, wi_lane0, n_meta, k_top)
    band = _band_tables(rel_bias, n_meta)
    o_dsa = _attention("dsa", dsa3, dsa3m, h_dsa, hd, 512, (mask, band))

    mixed = _branch_merge(o_fox.reshape(b * L, w_fox), o_dsa.reshape(b * L, w_dsa),
                          w_branch_fox[0].astype(BF16), w_branch_dsa[0].astype(BF16), gate, BF16)
    h2 = _mm(mixed, w_out[0].astype(BF16), F32, res=x2)

    u2 = _rmsnorm(h2, mlp_norm_g[0], BF16)
    a = _mm(u2, w_up[0].astype(BF16), BF16, act="relu2")
    h3 = _mm_acc(a, w_down[0].astype(BF16), h2, F32)
    return _rmsnorm(h3, final_norm_g, x.dtype).reshape(b, L, d)
```

```python
import functools
import math

import jax
import jax.numpy as jnp
from jax import lax
from jax.experimental import pallas as pl
from jax.experimental.pallas import tpu as pltpu

F32 = jnp.float32
BF16 = jnp.bfloat16

CHUNK = 64
H_IDX = 32
D_IDX = 64
TOPK_MAX = 256
MAX_DISTANCE = 128
RMS_EPS = 1e-6

LANES = 128
META_PAD = 128
NEG = -1e30
LOG2E = math.log2(math.e)
INT_MIN = -(2 ** 31)
KEY_NEG_INF = -2139095041
VMEM_LIMIT = 56 * 1024 * 1024
VMEM_LIMIT_BIG = 60 * 1024 * 1024


def _tile(dim, pref):
    if dim <= pref:
        return dim
    for t in range(pref - pref % LANES, 0, -LANES):
        if dim % t == 0:
            return t
    raise ValueError((dim, pref))


def _cparams(*sem, vmem=VMEM_LIMIT):
    return pltpu.CompilerParams(dimension_semantics=sem, vmem_limit_bytes=vmem)


def _rmsnorm_kernel(x_ref, g_ref, o_ref):
    x = x_ref[...].astype(F32)
    ms = jnp.mean(x * x, axis=-1, keepdims=True)
    y = x * lax.rsqrt(ms + RMS_EPS)
    o_ref[...] = (y * g_ref[...]).astype(o_ref.dtype)


def _rmsnorm(x, g, out_dtype):
    m, d = x.shape
    tr = _tile(m, 256)
    return pl.pallas_call(
        _rmsnorm_kernel,
        grid=(m // tr,),
        in_specs=[pl.BlockSpec((tr, d), lambda i: (i, 0)), pl.BlockSpec((1, d), lambda i: (0, 0))],
        out_specs=pl.BlockSpec((tr, d), lambda i: (i, 0)),
        out_shape=jax.ShapeDtypeStruct((m, d), out_dtype),
        compiler_params=_cparams("arbitrary"),
        name="rmsnorm",
    )(x, g.reshape(1, d).astype(F32))


def _split_cast_kernel(w_ref, *o_refs, bounds):
    for o_ref, (lo, width) in zip(o_refs, bounds):
        o_ref[...] = w_ref[:, lo:lo + width].astype(o_ref.dtype)


def _split_cast(w, bounds, tr=64):
    k, n = w.shape
    assert k % tr == 0
    return pl.pallas_call(
        functools.partial(_split_cast_kernel, bounds=bounds),
        grid=(k // tr,),
        in_specs=[pl.BlockSpec((tr, n), lambda i: (i, 0))],
        out_specs=[pl.BlockSpec((tr, width), lambda i: (i, 0)) for _, width in bounds],
        out_shape=[jax.ShapeDtypeStruct((k, width), BF16) for _, width in bounds],
        compiler_params=_cparams("arbitrary"),
        name="split_cast",
    )(w)


def _act(acc, act):
    if act == "sigmoid":
        return jax.nn.sigmoid(acc)
    if act == "relu2":
        r = jnp.maximum(acc, 0.0)
        return r * r
    return acc


def _mm_kernel(a_ref, b_ref, o_ref, *, act):
    acc = jnp.dot(a_ref[...], b_ref[...], preferred_element_type=F32)
    o_ref[...] = _act(acc, act).astype(o_ref.dtype)


def _mm_res_kernel(a_ref, b_ref, r_ref, o_ref):
    acc = jnp.dot(a_ref[...], b_ref[...], preferred_element_type=F32)
    o_ref[...] = (r_ref[...] + acc).astype(o_ref.dtype)


def _mm(a, b, out_dtype, act=None, res=None, tm=1024, tn=512):
    m, k = a.shape
    _, n = b.shape
    tm, tn = _tile(m, tm), _tile(n, tn)
    in_specs = [pl.BlockSpec((tm, k), lambda i, j: (i, 0)), pl.BlockSpec((k, tn), lambda i, j: (0, j))]
    args = [a, b]
    if res is None:
        body = functools.partial(_mm_kernel, act=act)
    else:
        body = _mm_res_kernel
        in_specs.append(pl.BlockSpec((tm, tn), lambda i, j: (i, j)))
        args.append(res)
    return pl.pallas_call(
        body,
        grid=(m // tm, n // tn),
        in_specs=in_specs,
        out_specs=pl.BlockSpec((tm, tn), lambda i, j: (i, j)),
        out_shape=jax.ShapeDtypeStruct((m, n), out_dtype),
        compiler_params=_cparams("arbitrary", "arbitrary"),
        name="matmul",
    )(*args)


def _mm_acc_kernel(a_ref, b_ref, r_ref, o_ref):
    kk = pl.program_id(2)
    part = jnp.dot(a_ref[...], b_ref[...], preferred_element_type=F32)

    @pl.when(kk == 0)
    def _():
        o_ref[...] = r_ref[...] + part

    @pl.when(kk > 0)
    def _():
        o_ref[...] += part


def _mm_acc(a, b, res, out_dtype, tm=1024, tn=1024, tk=4096):
    assert out_dtype == F32 and res.dtype == F32
    m, k = a.shape
    _, n = b.shape
    tm, tn, tk = _tile(m, tm), _tile(n, tn), _tile(k, tk)
    return pl.pallas_call(
        _mm_acc_kernel,
        grid=(m // tm, n // tn, k // tk),
        in_specs=[
            pl.BlockSpec((tm, tk), lambda i, j, kk: (i, kk)),
            pl.BlockSpec((tk, tn), lambda i, j, kk: (kk, j)),
            pl.BlockSpec((tm, tn), lambda i, j, kk: (i, j)),
        ],
        out_specs=pl.BlockSpec((tm, tn), lambda i, j, kk: (i, j)),
        out_shape=jax.ShapeDtypeStruct((m, n), out_dtype),
        compiler_params=_cparams("arbitrary", "arbitrary", "arbitrary", vmem=VMEM_LIMIT_BIG),
        name="matmul_acc",
    )(a, b, res)


def _branch_kernel(of_ref, od_ref, wf_ref, wd_ref, g0_ref, g1_ref, o_ref):
    yf = jnp.dot(of_ref[...], wf_ref[...], preferred_element_type=F32)
    yd = jnp.dot(od_ref[...], wd_ref[...], preferred_element_type=F32)
    o_ref[...] = (g0_ref[...] * yf + g1_ref[...] * yd).astype(o_ref.dtype)


def _branch_merge(o_fox, o_dsa, wf, wd, gate, out_dtype, tm=512, tn=1024):
    m, kf = o_fox.shape
    _, kd = o_dsa.shape
    n = wf.shape[1]
    tm, tn = _tile(m, tm), _tile(n, tn)
    nj = n // tn
    return pl.pallas_call(
        _branch_kernel,
        grid=(m // tm, nj),
        in_specs=[
            pl.BlockSpec((tm, kf), lambda i, j: (i, 0)),
            pl.BlockSpec((tm, kd), lambda i, j: (i, 0)),
            pl.BlockSpec((kf, tn), lambda i, j: (0, j)),
            pl.BlockSpec((kd, tn), lambda i, j: (0, j)),
            pl.BlockSpec((tm, tn), lambda i, j: (i, j)),
            pl.BlockSpec((tm, tn), lambda i, j: (i, j + nj)),
        ],
        out_specs=pl.BlockSpec((tm, tn), lambda i, j: (i, j)),
        out_shape=jax.ShapeDtypeStruct((m, n), out_dtype),
        compiler_params=_cparams("arbitrary", "arbitrary"),
        name="branch_merge",
    )(o_fox, o_dsa, wf, wd, gate, gate)


def _split3(x):
    hi = x.astype(BF16)
    r1 = x - hi.astype(F32)
    mid = r1.astype(BF16)
    lo = (r1 - mid.astype(F32)).astype(BF16)
    return hi, mid, lo


def _cumsum_lanes(xt, upper):
    out = None
    for part in _split3(xt):
        d = jnp.dot(part, upper, preferred_element_type=F32)
        out = d if out is None else out + d
    return out


def _decay_kernel(slab_ref, slabm_ref, bias_ref, dt_ref, dtm_ref, *, blk, lane0, nh, n_meta):
    L = slab_ref.shape[1]
    bias = bias_ref[...]

    def upper(n):
        r = lax.broadcasted_iota(jnp.int32, (n, n), 0)
        c = lax.broadcasted_iota(jnp.int32, (n, n), 1)
        return (r <= c).astype(BF16)

    xm = jax.nn.log_sigmoid(slabm_ref[...] + bias)
    dm = _cumsum_lanes(xm.T, upper(META_PAD))
    dtm_ref[0] = dm[lane0:lane0 + nh, :] * -LOG2E
    carry = dm[:, n_meta - 1:n_meta]
    up = upper(blk)
    for i in range(L // blk):
        x = jax.nn.log_sigmoid(slab_ref[0, i * blk:(i + 1) * blk, :] + bias)
        d = _cumsum_lanes(x.T, up) + carry
        dt_ref[0, :, i * blk:(i + 1) * blk] = d[lane0:lane0 + nh, :] * -LOG2E
        carry = d[:, blk - 1:blk]


def _decay(slab, slabm, bias_row, lane0, nh, n_meta):
    b, L, _ = slab.shape
    blk = _tile(L, 256)
    return pl.pallas_call(
        functools.partial(_decay_kernel, blk=blk, lane0=lane0, nh=nh, n_meta=n_meta),
        grid=(b,),
        in_specs=[
            pl.BlockSpec((1, L, LANES), lambda i: (i, 0, 0)),
            pl.BlockSpec((META_PAD, LANES), lambda i: (0, 0)),
            pl.BlockSpec((1, LANES), lambda i: (0, 0)),
        ],
        out_specs=[
            pl.BlockSpec((1, nh, L), lambda i: (i, 0, 0)),
            pl.BlockSpec((1, nh, META_PAD), lambda i: (i, 0, 0)),
        ],
        out_shape=[
            jax.ShapeDtypeStruct((b, nh, L), F32),
            jax.ShapeDtypeStruct((b, nh, META_PAD), F32),
        ],
        compiler_params=_cparams("arbitrary"),
        name="forget_cumsum",
    )(slab, slabm, bias_row)


def _dot_nt(a, b):
    return lax.dot_general(a, b, (((1,), (1,)), ((), ())), preferred_element_type=F32)


def _online_softmax_step(h, hd, s2, v, m_ref, l_ref, acc_ref):
    ts = s2.shape[1]
    m_prev = m_ref[h]
    m_new = jnp.maximum(m_prev, jnp.max(s2, axis=1, keepdims=True))
    alpha = jnp.exp2(m_prev - m_new)
    p = jnp.exp2(s2 - jnp.tile(m_new, (1, ts // LANES))).astype(BF16)
    v_ones = jnp.concatenate([v, jnp.ones((ts, LANES), v.dtype)], axis=1)
    pv = jnp.dot(p, v_ones, preferred_element_type=F32)
    hs = slice(h * hd, (h + 1) * hd)
    acc_ref[:, hs] = acc_ref[:, hs] * jnp.tile(alpha, (1, hd // LANES)) + pv[:, :hd]
    l_ref[h] = alpha * l_ref[h] + pv[:, hd:]
    m_ref[h] = m_new


def _attn_init(m_ref, l_ref, acc_ref):
    m_ref[...] = jnp.full_like(m_ref, NEG)
    l_ref[...] = jnp.zeros_like(l_ref)
    acc_ref[...] = jnp.zeros_like(acc_ref)


def _attn_finish(o_ref, l_ref, acc_ref, nh, hd):
    for h in range(nh):
        hs = slice(h * hd, (h + 1) * hd)
        o_ref[0, :, hs] = (acc_ref[:, hs] / jnp.tile(l_ref[h], (1, hd // LANES))).astype(o_ref.dtype)


def _fox_kernel(q_ref, k_ref, v_ref, km_ref, vm_ref, dt_ref, dtm_ref, o_ref, m_ref, l_ref, acc_ref,
                *, nh, hd, n_meta):
    qt, kt = pl.program_id(1), pl.program_id(2)
    tq, ts = q_ref.shape[1], k_ref.shape[1]
    scale = hd ** -0.5 * LOG2E

    @pl.when(kt == 0)
    def _():
        _attn_init(m_ref, l_ref, acc_ref)

    def frames(diag):
        if diag:
            row = lax.broadcasted_iota(jnp.int32, (tq, ts), 0)
            col = lax.broadcasted_iota(jnp.int32, (tq, ts), 1)
            causal = col <= row
            colm = lax.broadcasted_iota(jnp.int32, (tq, META_PAD), 1)
            pad = jnp.where(colm < n_meta, 0.0, NEG).astype(F32)
        for h in range(nh):
            hs = slice(h * hd, (h + 1) * hd)
            q = q_ref[0, :, hs]
            s = _dot_nt(q, k_ref[0, :, hs]) * scale + dt_ref[0, h:h + 1, :]
            v = v_ref[0, :, hs]
            if diag:
                sm = _dot_nt(q, km_ref[:, hs]) * scale + dtm_ref[0, h:h + 1, :] + pad
                s = jnp.concatenate([jnp.where(causal, s, NEG), sm], axis=1)
                v = jnp.concatenate([v, vm_ref[:, hs]], axis=0)
            _online_softmax_step(h, hd, s, v, m_ref, l_ref, acc_ref)

    @pl.when(kt < qt)
    def _():
        frames(False)

    @pl.when(kt == qt)
    def _():
        frames(True)
        _attn_finish(o_ref, l_ref, acc_ref, nh, hd)


def _add_blocks(s, adds):
    rows = []
    for ib in range(s.shape[0] // LANES):
        r = s[ib * LANES:(ib + 1) * LANES]
        if any(i == ib for i, _ in adds):
            r = jnp.concatenate(
                [r[:, jb * LANES:(jb + 1) * LANES] + adds[(ib, jb)] if (ib, jb) in adds
                 else r[:, jb * LANES:(jb + 1) * LANES] for jb in range(s.shape[1] // LANES)], axis=1)
        rows.append(r)
    return jnp.concatenate(rows, axis=0)


def _dsa_kernel(q_ref, k_ref, v_ref, km_ref, vm_ref, mask_ref, maskm_ref, band_ref, o_ref,
                m_ref, l_ref, acc_ref, *, nh, hd):
    qt, kt = pl.program_id(1), pl.program_id(2)
    tq, ts = q_ref.shape[1], k_ref.shape[1]
    nb = tq // LANES
    scale = hd ** -0.5 * LOG2E

    @pl.when(kt == 0)
    def _():
        _attn_init(m_ref, l_ref, acc_ref)

    def frames(kind):
        maskf = mask_ref[0].astype(F32)
        if kind == "diag":
            maskm = maskm_ref[0].astype(F32)
            first = (qt == 0).astype(F32)
        for h in range(nh):
            hs = slice(h * hd, (h + 1) * hd)
            q = q_ref[0, :, hs]
            s = _dot_nt(q, k_ref[0, :, hs]) * scale + maskf
            v = v_ref[0, :, hs]
            if kind == "near":
                s = _add_blocks(s, {(0, ts // LANES - 1): band_ref[1, h]})
            elif kind == "diag":
                sm = _dot_nt(q, km_ref[:, hs]) * scale + maskm
                s = jnp.concatenate([s, sm], axis=1)
                adds = {(0, ts // LANES): band_ref[2, h] * first}
                for ib in range(nb):
                    adds[(ib, ib)] = band_ref[0, h]
                    if ib > 0:
                        adds[(ib, ib - 1)] = band_ref[1, h]
                s = _add_blocks(s, adds)
                v = jnp.concatenate([v, vm_ref[:, hs]], axis=0)
            _online_softmax_step(h, hd, s, v, m_ref, l_ref, acc_ref)

    @pl.when(kt < qt - 1)
    def _():
        frames("far")

    @pl.when(kt == qt - 1)
    def _():
        frames("near")

    @pl.when(kt == qt)
    def _():
        frames("diag")
        _attn_finish(o_ref, l_ref, acc_ref, nh, hd)


def _attention(kind, big, bigm, nh, hd, tile, extra):
    b, L, _ = big.shape
    w = nh * hd
    q_blk, k_blk, v_blk = 0, 1, 2
    t = _tile(L, tile)
    nq = L // t
    q_spec = pl.BlockSpec((1, t, w), lambda bi, qi, ki: (bi, qi, q_blk))
    kv = lambda blk: pl.BlockSpec((1, t, w), lambda bi, qi, ki: (bi, jnp.minimum(ki, qi), blk))
    kvm = lambda blk: pl.BlockSpec((META_PAD, w), lambda bi, qi, ki: (0, blk))
    scratch = [
        pltpu.VMEM((nh, t, LANES), F32),
        pltpu.VMEM((nh, t, LANES), F32),
        pltpu.VMEM((t, w), F32),
    ]
    if kind == "fox":
        dt, dtm, n_meta = extra
        body = functools.partial(_fox_kernel, nh=nh, hd=hd, n_meta=n_meta)
        in_specs = [q_spec, kv(k_blk), kv(v_blk), kvm(k_blk), kvm(v_blk),
                    pl.BlockSpec((1, nh, t), lambda bi, qi, ki: (bi, 0, jnp.minimum(ki, qi))),
                    pl.BlockSpec((1, nh, META_PAD), lambda bi, qi, ki: (bi, 0, 0))]
        args = [big, big, big, bigm, bigm, dt, dtm]
    else:
        mask, band = extra
        body = functools.partial(_dsa_kernel, nh=nh, hd=hd)
        in_specs = [q_spec, kv(k_blk), kv(v_blk), kvm(k_blk), kvm(v_blk),
                    pl.BlockSpec((1, t, t), lambda bi, qi, ki: (bi, qi, jnp.minimum(ki, qi))),
                    pl.BlockSpec((1, t, META_PAD), lambda bi, qi, ki: (bi, qi, L // META_PAD)),
                    pl.BlockSpec((3, nh, LANES, LANES), lambda bi, qi, ki: (0, 0, 0, 0))]
        args = [big, big, big, bigm, bigm, mask, mask, band]
    return pl.pallas_call(
        body,
        grid=(b, nq, nq),
        in_specs=in_specs,
        out_specs=pl.BlockSpec((1, t, w), lambda bi, qi, ki: (bi, qi, 0)),
        out_shape=jax.ShapeDtypeStruct((b, L, w), BF16),
        scratch_shapes=scratch,
        compiler_params=_cparams("arbitrary", "arbitrary", "arbitrary"),
        name=kind + "_attention",
    )(*args)


def _sortable(x):
    bits = pltpu.bitcast(x, jnp.int32)
    return bits ^ ((bits >> 31) & 0x7FFFFFFF)


def _index_kernel(qi_ref, w_ref, kie_ref, kio_ref, kiem_ref, kiom_ref, mask_ref, wb_ref, st_ref, smt_ref,
                  *, tc, w_lane0, n_meta, k_top):
    qt = pl.program_id(1)
    tq = qi_ref.shape[1]
    L = kie_ref.shape[1]
    n_ct = (qt + 1) * tq // tc
    npair = H_IDX // 2
    wscale = (H_IDX ** -0.5) * (D_IDX ** -0.5)

    wv = w_ref[0]
    for h in range(H_IDX):
        col = wv[:, w_lane0 + h:w_lane0 + h + 1] * wscale
        wb_ref[h] = jnp.broadcast_to(col, (tq, LANES))

    def score_tile(ke, ko):
        n = ke.shape[0]
        accs = [jnp.zeros((tq, LANES), F32) for _ in range(n // LANES)]
        for j in range(npair):
            lhs = qi_ref[0, :, j * LANES:(j + 1) * LANES]
            de = jnp.maximum(_dot_nt(lhs, ke), 0.0)
            do = jnp.maximum(_dot_nt(lhs, ko), 0.0)
            for c in range(n // LANES):
                cs = slice(c * LANES, (c + 1) * LANES)
                accs[c] = accs[c] + wb_ref[2 * j] * de[:, cs] + wb_ref[2 * j + 1] * do[:, cs]
        return accs[0] if len(accs) == 1 else jnp.concatenate(accs, axis=1)

    colm = lax.broadcasted_iota(jnp.int32, (tq, META_PAD), 1)
    sm = jnp.where(colm < n_meta, score_tile(kiem_ref[...], kiom_ref[...]), -jnp.inf)
    smt_ref[...] = _sortable(sm.T)

    qchunk = (qt * tq + lax.broadcasted_iota(jnp.int32, (tq, tc), 0)) // CHUNK

    def tile_body(c, carry):
        start = pl.multiple_of(c * tc, tc)
        sc = score_tile(kie_ref[0, pl.ds(start, tc), :], kio_ref[0, pl.ds(start, tc), :])
        kchunk = (start + lax.broadcasted_iota(jnp.int32, (tq, tc), 1)) // CHUNK
        sc = jnp.where(kchunk <= qchunk, sc, -jnp.inf)
        st_ref[c] = _sortable(sc.T)
        return carry

    lax.fori_loop(0, n_ct, tile_body, 0)

    sub = 8
    nacc = 4

    def count_tile(tile, thr8, accs):
        for r in range(tile.shape[0] // sub):
            accs[r % nacc] = accs[r % nacc] + (tile[r * sub:(r + 1) * sub, :] >= thr8).astype(jnp.int32)
        return accs

    def count_ge(thr8):
        accs = count_tile(smt_ref[...], thr8, [jnp.zeros((sub, tq), jnp.int32) for _ in range(nacc)])
        accs = lax.fori_loop(0, n_ct, lambda c, a: tuple(count_tile(st_ref[c], thr8, list(a))), tuple(accs))
        total = functools.reduce(lambda x, y: x + y, accs)
        return jnp.broadcast_to(jnp.sum(total, axis=0, keepdims=True), (sub, tq))

    def bit_body(it, ans_u):
        trial_u = ans_u | (jnp.int32(1) << (31 - it))
        cnt = count_ge(trial_u ^ INT_MIN)
        return jnp.where(cnt >= k_top, trial_u, ans_u)

    ans_u = lax.fori_loop(0, 32, bit_body, jnp.zeros((sub, tq), jnp.int32))
    thr = jnp.maximum(ans_u ^ INT_MIN, KEY_NEG_INF + 1)[:1]

    def mask_of(keys_t):
        return jnp.where(keys_t >= thr, 0.0, NEG).astype(F32).T.astype(mask_ref.dtype)

    mask_ref[0, :, L:] = mask_of(smt_ref[...])
    for c in range(L // tc):
        cols = slice(c * tc, (c + 1) * tc)

        @pl.when(c < n_ct)
        def _():
            mask_ref[0, :, cols] = mask_of(st_ref[c])

        @pl.when(c >= n_ct)
        def _():
            mask_ref[0, :, cols] = jnp.full((tq, tc), NEG, mask_ref.dtype)


def _indexer_mask(qi, slab, kie, kio, kiem, kiom, w_lane0, n_meta, k_top, tq=256, tc=256):
    b, L, _ = qi.shape
    tq = _tile(L, tq)
    tc = _tile(tq, tc)
    wq = H_IDX * D_IDX
    return pl.pallas_call(
        functools.partial(_index_kernel, tc=tc, w_lane0=w_lane0, n_meta=n_meta, k_top=k_top),
        grid=(b, L // tq),
        in_specs=[
            pl.BlockSpec((1, tq, wq), lambda bi, qi: (bi, qi, 0)),
            pl.BlockSpec((1, tq, LANES), lambda bi, qi: (bi, qi, 0)),
            pl.BlockSpec((1, L, LANES), lambda bi, qi: (bi, 0, 0)),
            pl.BlockSpec((1, L, LANES), lambda bi, qi: (bi, 0, 0)),
            pl.BlockSpec((META_PAD, LANES), lambda bi, qi: (0, 0)),
            pl.BlockSpec((META_PAD, LANES), lambda bi, qi: (0, 0)),
        ],
        out_specs=pl.BlockSpec((1, tq, L + META_PAD), lambda bi, qi: (bi, qi, 0)),
        out_shape=jax.ShapeDtypeStruct((b, L, L + META_PAD), BF16),
        scratch_shapes=[
            pltpu.VMEM((H_IDX, tq, LANES), F32),
            pltpu.VMEM((L // tc, tc, tq), jnp.int32),
            pltpu.VMEM((META_PAD, tq), jnp.int32),
        ],
        compiler_params=_cparams("arbitrary", "arbitrary"),
        name="indexer_topk_mask",
    )(qi, slab, kie, kio, kiem, kiom)


def _t5_bucket(rel, n_buckets):
    half = n_buckets // 2
    max_exact = half // 2
    ret = jnp.where(rel > 0, half, 0)
    n = jnp.abs(rel)
    nf = jnp.maximum(n, 1).astype(jnp.float32)
    large = max_exact + (jnp.log(nf / max_exact) / math.log(MAX_DISTANCE / max_exact)
                         * (half - max_exact)).astype(jnp.int32)
    large = jnp.minimum(large, half - 1)
    return ret + jnp.where(n < max_exact, n, large)


def _band_kernel(bucket_ref, far_ref, rb_ref, o_ref, *, n_buckets, nh):
    bucket = bucket_ref[0]
    far = far_ref[0]
    for h in range(nh):
        acc = jnp.zeros(bucket.shape, F32)
        for bk in range(n_buckets):
            acc = jnp.where(bucket == bk, rb_ref[bk, h], acc)
        o_ref[0, h] = (acc - rb_ref[far, h]) * LOG2E


def _band_tables(rel_bias, n_meta):
    n_buckets, nh = rel_bias.shape
    i = jnp.arange(LANES, dtype=jnp.int32)[:, None]
    j = jnp.arange(LANES, dtype=jnp.int32)[None, :]
    rel = jnp.stack([j - i, j - i - LANES, j - n_meta - i])
    bucket = _t5_bucket(rel, n_buckets).astype(jnp.int32)
    far = _t5_bucket(jnp.full((1,), -MAX_DISTANCE, jnp.int32), n_buckets).astype(jnp.int32)
    return pl.pallas_call(
        functools.partial(_band_kernel, n_buckets=n_buckets, nh=nh),
        grid=(3,),
        in_specs=[
            pl.BlockSpec((1, LANES, LANES), lambda t: (t, 0, 0)),
            pl.BlockSpec(memory_space=pltpu.SMEM),
            pl.BlockSpec(memory_space=pltpu.SMEM),
        ],
        out_specs=pl.BlockSpec((1, nh, LANES, LANES), lambda t: (t, 0, 0, 0)),
        out_shape=jax.ShapeDtypeStruct((3, nh, LANES, LANES), F32),
        compiler_params=_cparams("arbitrary"),
        name="rel_bias_band",
    )(bucket, far, rel_bias.astype(F32))


def kernel(x, meta_tokens, attn_norm_g, w_in, forget_bias, rel_bias, w_branch_fox, w_branch_dsa,
           w_out, mlp_norm_g, w_up, w_down, final_norm_g):
    b, L, d = x.shape
    n_meta = meta_tokens.shape[0]
    h_fox = forget_bias.shape[1]
    h_dsa = rel_bias.shape[1]
    w_fox, w_dsa = w_branch_fox.shape[1], w_branch_dsa.shape[1]
    hd = w_fox // h_fox
    w_idx = H_IDX * D_IDX
    assert attn_norm_g.shape[0] == 1, "single-layer block"
    assert w_dsa // h_dsa == hd and hd % LANES == 0
    assert n_meta <= META_PAD and L % CHUNK == 0 and L % META_PAD == 0
    assert MAX_DISTANCE <= LANES and CHUNK <= LANES
    k_top = min(TOPK_MAX, L // 4)

    wi = w_in[0]
    o = 0
    cols = {}
    for name, width in (("qa", w_fox), ("ka", w_fox), ("va", w_fox), ("fa", h_fox), ("qb", w_dsa),
                        ("kb", w_dsa), ("vb", w_dsa), ("qi", w_idx), ("ki", D_IDX), ("wi", H_IDX),
                        ("gl", 2 * d)):
        cols[name] = wi[:, o:o + width]
        o += width
    assert o == wi.shape[1]
    off_dsa = 3 * w_fox + h_fox
    off_qi = off_dsa + 3 * w_dsa
    off_gate = off_qi + w_idx + D_IDX + H_IDX
    w_fox3, w_dsa3, w_qi, w_gate = _split_cast(
        wi, ((0, 3 * w_fox), (off_dsa, 3 * w_dsa), (off_qi, w_idx), (off_gate, 2 * d)))
    n_small = D_IDX + H_IDX + h_fox
    assert n_small <= LANES
    w_small = jnp.concatenate([cols["ki"], cols["wi"], cols["fa"],
                               jnp.zeros((d, LANES - n_small), wi.dtype)], axis=1).astype(BF16)
    wi_lane0, fa_lane0 = D_IDX, D_IDX + H_IDX

    x2 = x.reshape(b * L, d)
    meta = jnp.zeros((META_PAD, d), x.dtype).at[:n_meta].set(meta_tokens.astype(x.dtype))
    u = _rmsnorm(x2, attn_norm_g[0], BF16)
    um = _rmsnorm(meta, attn_norm_g[0], BF16)
    fox3 = _mm(u, w_fox3, BF16).reshape(b, L, -1)
    fox3m = _mm(um, w_fox3, BF16)
    dsa3 = _mm(u, w_dsa3, BF16).reshape(b, L, -1)
    dsa3m = _mm(um, w_dsa3, BF16)
    qi = _mm(u, w_qi, BF16).reshape(b, L, -1)
    gate = _mm(u, w_gate, F32, act="sigmoid")
    slab = _mm(u, w_small, F32, tn=LANES).reshape(b, L, LANES)
    slabm = _mm(um, w_small, F32, tn=LANES)

    bias_row = jnp.zeros((1, LANES), F32).at[0, fa_lane0:fa_lane0 + h_fox].set(forget_bias[0].astype(F32))
    dt, dtm = _decay(slab, slabm, bias_row, fa_lane0, h_fox, n_meta)
    o_fox = _attention("fox", fox3, fox3m, h_fox, hd, 512, (dt, dtm, n_meta))

    zpad = jnp.zeros(slab.shape[:-1] + (LANES - D_IDX,), BF16)
    ki = slab[..., :D_IDX].astype(BF16)
    kie, kio = jnp.concatenate([ki, zpad], axis=-1), jnp.concatenate([zpad, ki], axis=-1)
    kim = slabm[:, :D_IDX].astype(BF16)
    kiem = jnp.concatenate([kim, zpad[0, :META_PAD]], axis=-1)
    kiom = jnp.concatenate([zpad[0, :META_PAD], kim], axis=-1)
    mask = _indexer_mask(qi, slab, kie, kio, kiem, kiom, wi_lane0, n_meta, k_top)
    band = _band_tables(rel_bias, n_meta)
    o_dsa = _attention("dsa", dsa3, dsa3m, h_dsa, hd, 512, (mask, band))

    mixed = _branch_merge(o_fox.reshape(b * L, w_fox), o_dsa.reshape(b * L, w_dsa),
                          w_branch_fox[0].astype(BF16), w_branch_dsa[0].astype(BF16), gate, BF16)
    h2 = _mm(mixed, w_out[0].astype(BF16), F32, res=x2)

    u2 = _rmsnorm(h2, mlp_norm_g[0], BF16)
    a = _mm(u2, w_up[0].astype(BF16), BF16, act="relu2")
    h3 = _mm_acc(a, w_down[0].astype(BF16), h2, F32)
    return _rmsnorm(h3, final_norm_g, x.dtype).reshape(b, L, d)
```

```python
import functools
import math

import jax
import jax.numpy as jnp
from jax import lax
from jax.experimental import pallas as pl
from jax.experimental.pallas import tpu as pltpu

F32 = jnp.float32
BF16 = jnp.bfloat16

CHUNK = 64
H_IDX = 32
D_IDX = 64
TOPK_MAX = 256
MAX_DISTANCE = 128
RMS_EPS = 1e-6

LANES = 128
SUBLANES = 8
META_PAD = 128
NEG = -1e30
LOG2E = math.log2(math.e)
INT_MIN = -(2 ** 31)
KEY_NEG_INF = -2139095041
VMEM_LIMIT = 56 * 1024 * 1024
VMEM_LIMIT_BIG = 60 * 1024 * 1024


def _tile(dim, pref):
    if dim <= pref:
        return dim
    for t in range(pref - pref % LANES, 0, -LANES):
        if dim % t == 0:
            return t
    raise ValueError((dim, pref))


def _cparams(*sem, vmem=VMEM_LIMIT):
    return pltpu.CompilerParams(dimension_semantics=sem, vmem_limit_bytes=vmem)


def _rmsnorm_kernel(x_ref, g_ref, o_ref):
    x = x_ref[...].astype(F32)
    ms = jnp.mean(x * x, axis=-1, keepdims=True)
    y = x * lax.rsqrt(ms + RMS_EPS)
    o_ref[...] = (y * g_ref[...]).astype(o_ref.dtype)


def _rmsnorm(x, g, out_dtype):
    m, d = x.shape
    tr = _tile(m, 256)
    return pl.pallas_call(
        _rmsnorm_kernel,
        grid=(m // tr,),
        in_specs=[pl.BlockSpec((tr, d), lambda i: (i, 0)), pl.BlockSpec((1, d), lambda i: (0, 0))],
        out_specs=pl.BlockSpec((tr, d), lambda i: (i, 0)),
        out_shape=jax.ShapeDtypeStruct((m, d), out_dtype),
        compiler_params=_cparams("arbitrary"),
        name="rmsnorm",
    )(x, g.reshape(1, d).astype(F32))


def _columns_kernel(*refs):
    *w_refs, o_ref = refs
    parts = [r[...] for r in w_refs]
    pad = o_ref.shape[1] - sum(p.shape[0] for p in parts)
    if pad:
        parts.append(jnp.zeros((pad, parts[0].shape[1]), parts[0].dtype))
    rows = parts[0] if len(parts) == 1 else jnp.concatenate(parts, axis=0)
    o_ref[...] = rows.T.astype(o_ref.dtype)


def _columns_bf16(wt, ranges, chunk=512):
    n, k = wt.shape
    if len(ranges) == 1 and ranges[0][1] > chunk:
        (lo, width), = ranges
        chunk = _tile(width, chunk)
        steps, widths, out_w = width // chunk, [chunk], chunk
        offs = [lo]
    else:
        steps, widths = 1, [w for _, w in ranges]
        offs = [lo for lo, _ in ranges]
        out_w = -(-sum(widths) // LANES) * LANES
    assert all(o % SUBLANES == 0 and w % SUBLANES == 0 for o, w in zip(offs, widths))
    in_specs = [pl.BlockSpec((pl.Element(w), pl.Element(k)),
                             functools.partial(lambda i, o, w: (pl.multiple_of(o + i * w, SUBLANES), 0), o=o, w=w))
                for o, w in zip(offs, widths)]
    return pl.pallas_call(
        _columns_kernel,
        grid=(steps,),
        in_specs=in_specs,
        out_specs=pl.BlockSpec((k, out_w), lambda i: (0, i)),
        out_shape=jax.ShapeDtypeStruct((k, out_w * steps), BF16),
        compiler_params=_cparams("arbitrary"),
        name="weight_columns",
    )(*([wt] * len(offs)))


def _act(acc, act):
    if act == "sigmoid":
        return jax.nn.sigmoid(acc)
    if act == "relu2":
        r = jnp.maximum(acc, 0.0)
        return r * r
    return acc


def _mm_kernel(a_ref, b_ref, o_ref, *, act):
    acc = jnp.dot(a_ref[...], b_ref[...], preferred_element_type=F32)
    o_ref[...] = _act(acc, act).astype(o_ref.dtype)


def _mm_res_kernel(a_ref, b_ref, r_ref, o_ref):
    acc = jnp.dot(a_ref[...], b_ref[...], preferred_element_type=F32)
    o_ref[...] = (r_ref[...] + acc).astype(o_ref.dtype)


def _mm(a, b, out_dtype, act=None, res=None, tm=1024, tn=512):
    m, k = a.shape
    _, n = b.shape
    tm, tn = _tile(m, tm), _tile(n, tn)
    in_specs = [pl.BlockSpec((tm, k), lambda i, j: (i, 0)), pl.BlockSpec((k, tn), lambda i, j: (0, j))]
    args = [a, b]
    if res is None:
        body = functools.partial(_mm_kernel, act=act)
    else:
        body = _mm_res_kernel
        in_specs.append(pl.BlockSpec((tm, tn), lambda i, j: (i, j)))
        args.append(res)
    return pl.pallas_call(
        body,
        grid=(m // tm, n // tn),
        in_specs=in_specs,
        out_specs=pl.BlockSpec((tm, tn), lambda i, j: (i, j)),
        out_shape=jax.ShapeDtypeStruct((m, n), out_dtype),
        compiler_params=_cparams("arbitrary", "arbitrary"),
        name="matmul",
    )(*args)


def _mm_acc_kernel(a_ref, b_ref, r_ref, o_ref):
    kk = pl.program_id(2)
    part = jnp.dot(a_ref[...], b_ref[...], preferred_element_type=F32)

    @pl.when(kk == 0)
    def _():
        o_ref[...] = r_ref[...] + part

    @pl.when(kk > 0)
    def _():
        o_ref[...] += part


def _mm_acc(a, b, res, out_dtype, tm=1024, tn=1024, tk=4096):
    assert out_dtype == F32 and res.dtype == F32
    m, k = a.shape
    _, n = b.shape
    tm, tn, tk = _tile(m, tm), _tile(n, tn), _tile(k, tk)
    return pl.pallas_call(
        _mm_acc_kernel,
        grid=(m // tm, n // tn, k // tk),
        in_specs=[
            pl.BlockSpec((tm, tk), lambda i, j, kk: (i, kk)),
            pl.BlockSpec((tk, tn), lambda i, j, kk: (kk, j)),
            pl.BlockSpec((tm, tn), lambda i, j, kk: (i, j)),
        ],
        out_specs=pl.BlockSpec((tm, tn), lambda i, j, kk: (i, j)),
        out_shape=jax.ShapeDtypeStruct((m, n), out_dtype),
        compiler_params=_cparams("arbitrary", "arbitrary", "arbitrary", vmem=VMEM_LIMIT_BIG),
        name="matmul_acc",
    )(a, b, res)


def _branch_kernel(of_ref, od_ref, wf_ref, wd_ref, g0_ref, g1_ref, o_ref):
    yf = jnp.dot(of_ref[...], wf_ref[...], preferred_element_type=F32)
    yd = jnp.dot(od_ref[...], wd_ref[...], preferred_element_type=F32)
    o_ref[...] = (g0_ref[...] * yf + g1_ref[...] * yd).astype(o_ref.dtype)


def _branch_merge(o_fox, o_dsa, wf, wd, gate, out_dtype, tm=512, tn=1024):
    m, kf = o_fox.shape
    _, kd = o_dsa.shape
    n = wf.shape[1]
    tm, tn = _tile(m, tm), _tile(n, tn)
    nj = n // tn
    return pl.pallas_call(
        _branch_kernel,
        grid=(m // tm, nj),
        in_specs=[
            pl.BlockSpec((tm, kf), lambda i, j: (i, 0)),
            pl.BlockSpec((tm, kd), lambda i, j: (i, 0)),
            pl.BlockSpec((kf, tn), lambda i, j: (0, j)),
            pl.BlockSpec((kd, tn), lambda i, j: (0, j)),
            pl.BlockSpec((tm, tn), lambda i, j: (i, j)),
            pl.BlockSpec((tm, tn), lambda i, j: (i, j + nj)),
        ],
        out_specs=pl.BlockSpec((tm, tn), lambda i, j: (i, j)),
        out_shape=jax.ShapeDtypeStruct((m, n), out_dtype),
        compiler_params=_cparams("arbitrary", "arbitrary"),
        name="branch_merge",
    )(o_fox, o_dsa, wf, wd, gate, gate)


def _split3(x):
    hi = x.astype(BF16)
    r1 = x - hi.astype(F32)
    mid = r1.astype(BF16)
    lo = (r1 - mid.astype(F32)).astype(BF16)
    return hi, mid, lo


def _cumsum_lanes(xt, upper):
    out = None
    for part in _split3(xt):
        d = jnp.dot(part, upper, preferred_element_type=F32)
        out = d if out is None else out + d
    return out


def _decay_kernel(slab_ref, slabm_ref, bias_ref, dt_ref, dtm_ref, *, blk, lane0, nh, n_meta):
    L = slab_ref.shape[1]
    bias = bias_ref[...]

    def upper(n):
        r = lax.broadcasted_iota(jnp.int32, (n, n), 0)
        c = lax.broadcasted_iota(jnp.int32, (n, n), 1)
        return (r <= c).astype(BF16)

    xm = jax.nn.log_sigmoid(slabm_ref[...] + bias)
    dm = _cumsum_lanes(xm.T, upper(META_PAD))
    dtm_ref[0] = dm[lane0:lane0 + nh, :] * -LOG2E
    carry = dm[:, n_meta - 1:n_meta]
    up = upper(blk)
    for i in range(L // blk):
        x = jax.nn.log_sigmoid(slab_ref[0, i * blk:(i + 1) * blk, :] + bias)
        d = _cumsum_lanes(x.T, up) + carry
        dt_ref[0, :, i * blk:(i + 1) * blk] = d[lane0:lane0 + nh, :] * -LOG2E
        carry = d[:, blk - 1:blk]


def _decay(slab, slabm, bias_row, lane0, nh, n_meta):
    b, L, _ = slab.shape
    blk = _tile(L, 256)
    return pl.pallas_call(
        functools.partial(_decay_kernel, blk=blk, lane0=lane0, nh=nh, n_meta=n_meta),
        grid=(b,),
        in_specs=[
            pl.BlockSpec((1, L, LANES), lambda i: (i, 0, 0)),
            pl.BlockSpec((META_PAD, LANES), lambda i: (0, 0)),
            pl.BlockSpec((1, LANES), lambda i: (0, 0)),
        ],
        out_specs=[
            pl.BlockSpec((1, nh, L), lambda i: (i, 0, 0)),
            pl.BlockSpec((1, nh, META_PAD), lambda i: (i, 0, 0)),
        ],
        out_shape=[
            jax.ShapeDtypeStruct((b, nh, L), F32),
            jax.ShapeDtypeStruct((b, nh, META_PAD), F32),
        ],
        compiler_params=_cparams("arbitrary"),
        name="forget_cumsum",
    )(slab, slabm, bias_row)


def _dot_nt(a, b):
    return lax.dot_general(a, b, (((1,), (1,)), ((), ())), preferred_element_type=F32)


def _online_softmax_step(h, hd, s2, v, m_ref, l_ref, acc_ref):
    ts = s2.shape[1]
    m_prev = m_ref[h]
    m_new = jnp.maximum(m_prev, jnp.max(s2, axis=1, keepdims=True))
    alpha = jnp.exp2(m_prev - m_new)
    p = jnp.exp2(s2 - jnp.tile(m_new, (1, ts // LANES))).astype(BF16)
    v_ones = jnp.concatenate([v, jnp.ones((ts, LANES), v.dtype)], axis=1)
    pv = jnp.dot(p, v_ones, preferred_element_type=F32)
    hs = slice(h * hd, (h + 1) * hd)
    acc_ref[:, hs] = acc_ref[:, hs] * jnp.tile(alpha, (1, hd // LANES)) + pv[:, :hd]
    l_ref[h] = alpha * l_ref[h] + pv[:, hd:]
    m_ref[h] = m_new


def _attn_init(m_ref, l_ref, acc_ref):
    m_ref[...] = jnp.full_like(m_ref, NEG)
    l_ref[...] = jnp.zeros_like(l_ref)
    acc_ref[...] = jnp.zeros_like(acc_ref)


def _attn_finish(o_ref, l_ref, acc_ref, nh, hd):
    for h in range(nh):
        hs = slice(h * hd, (h + 1) * hd)
        o_ref[0, :, hs] = (acc_ref[:, hs] / jnp.tile(l_ref[h], (1, hd // LANES))).astype(o_ref.dtype)


def _fox_kernel(q_ref, k_ref, v_ref, km_ref, vm_ref, dt_ref, dtm_ref, o_ref, m_ref, l_ref, acc_ref,
                *, nh, hd, n_meta):
    qt, kt = pl.program_id(1), pl.program_id(2)
    tq, ts = q_ref.shape[1], k_ref.shape[1]
    scale = hd ** -0.5 * LOG2E

    @pl.when(kt == 0)
    def _():
        _attn_init(m_ref, l_ref, acc_ref)

    def frames(diag):
        if diag:
            row = lax.broadcasted_iota(jnp.int32, (tq, ts), 0)
            col = lax.broadcasted_iota(jnp.int32, (tq, ts), 1)
            causal = col <= row
            colm = lax.broadcasted_iota(jnp.int32, (tq, META_PAD), 1)
            pad = jnp.where(colm < n_meta, 0.0, NEG).astype(F32)
        for h in range(nh):
            hs = slice(h * hd, (h + 1) * hd)
            q = q_ref[0, :, hs]
            s = _dot_nt(q, k_ref[0, :, hs]) * scale + dt_ref[0, h:h + 1, :]
            v = v_ref[0, :, hs]
            if diag:
                sm = _dot_nt(q, km_ref[:, hs]) * scale + dtm_ref[0, h:h + 1, :] + pad
                s = jnp.concatenate([jnp.where(causal, s, NEG), sm], axis=1)
                v = jnp.concatenate([v, vm_ref[:, hs]], axis=0)
            _online_softmax_step(h, hd, s, v, m_ref, l_ref, acc_ref)

    @pl.when(kt < qt)
    def _():
        frames(False)

    @pl.when(kt == qt)
    def _():
        frames(True)
        _attn_finish(o_ref, l_ref, acc_ref, nh, hd)


def _add_blocks(s, adds):
    rows = []
    for ib in range(s.shape[0] // LANES):
        r = s[ib * LANES:(ib + 1) * LANES]
        if any(i == ib for i, _ in adds):
            r = jnp.concatenate(
                [r[:, jb * LANES:(jb + 1) * LANES] + adds[(ib, jb)] if (ib, jb) in adds
                 else r[:, jb * LANES:(jb + 1) * LANES] for jb in range(s.shape[1] // LANES)], axis=1)
        rows.append(r)
    return jnp.concatenate(rows, axis=0)


def _dsa_kernel(q_ref, k_ref, v_ref, km_ref, vm_ref, mask_ref, maskm_ref, band_ref, o_ref,
                m_ref, l_ref, acc_ref, *, nh, hd):
    qt, kt = pl.program_id(1), pl.program_id(2)
    tq, ts = q_ref.shape[1], k_ref.shape[1]
    nb = tq // LANES
    scale = hd ** -0.5 * LOG2E

    @pl.when(kt == 0)
    def _():
        _attn_init(m_ref, l_ref, acc_ref)

    def frames(kind):
        maskf = mask_ref[0].astype(F32)
        if kind == "diag":
            maskm = maskm_ref[0].astype(F32)
            first = (qt == 0).astype(F32)
        for h in range(nh):
            hs = slice(h * hd, (h + 1) * hd)
            q = q_ref[0, :, hs]
            s = _dot_nt(q, k_ref[0, :, hs]) * scale + maskf
            v = v_ref[0, :, hs]
            if kind == "near":
                s = _add_blocks(s, {(0, ts // LANES - 1): band_ref[1, h]})
            elif kind == "diag":
                sm = _dot_nt(q, km_ref[:, hs]) * scale + maskm
                s = jnp.concatenate([s, sm], axis=1)
                adds = {(0, ts // LANES): band_ref[2, h] * first}
                for ib in range(nb):
                    adds[(ib, ib)] = band_ref[0, h]
                    if ib > 0:
                        adds[(ib, ib - 1)] = band_ref[1, h]
                s = _add_blocks(s, adds)
                v = jnp.concatenate([v, vm_ref[:, hs]], axis=0)
            _online_softmax_step(h, hd, s, v, m_ref, l_ref, acc_ref)

    @pl.when(kt < qt - 1)
    def _():
        frames("far")

    @pl.when(kt == qt - 1)
    def _():
        frames("near")

    @pl.when(kt == qt)
    def _():
        frames("diag")
        _attn_finish(o_ref, l_ref, acc_ref, nh, hd)


def _attention(kind, big, bigm, nh, hd, tile, extra):
    b, L, _ = big.shape
    w = nh * hd
    q_blk, k_blk, v_blk = 0, 1, 2
    t = _tile(L, tile)
    nq = L // t
    q_spec = pl.BlockSpec((1, t, w), lambda bi, qi, ki: (bi, qi, q_blk))
    kv = lambda blk: pl.BlockSpec((1, t, w), lambda bi, qi, ki: (bi, jnp.minimum(ki, qi), blk))
    kvm = lambda blk: pl.BlockSpec((META_PAD, w), lambda bi, qi, ki: (0, blk))
    scratch = [
        pltpu.VMEM((nh, t, LANES), F32),
        pltpu.VMEM((nh, t, LANES), F32),
        pltpu.VMEM((t, w), F32),
    ]
    if kind == "fox":
        dt, dtm, n_meta = extra
        body = functools.partial(_fox_kernel, nh=nh, hd=hd, n_meta=n_meta)
        in_specs = [q_spec, kv(k_blk), kv(v_blk), kvm(k_blk), kvm(v_blk),
                    pl.BlockSpec((1, nh, t), lambda bi, qi, ki: (bi, 0, jnp.minimum(ki, qi))),
                    pl.BlockSpec((1, nh, META_PAD), lambda bi, qi, ki: (bi, 0, 0))]
        args = [big, big, big, bigm, bigm, dt, dtm]
    else:
        mask, band = extra
        body = functools.partial(_dsa_kernel, nh=nh, hd=hd)
        in_specs = [q_spec, kv(k_blk), kv(v_blk), kvm(k_blk), kvm(v_blk),
                    pl.BlockSpec((1, t, t), lambda bi, qi, ki: (bi, qi, jnp.minimum(ki, qi))),
                    pl.BlockSpec((1, t, META_PAD), lambda bi, qi, ki: (bi, qi, L // META_PAD)),
                    pl.BlockSpec((3, nh, LANES, LANES), lambda bi, qi, ki: (0, 0, 0, 0))]
        args = [big, big, big, bigm, bigm, mask, mask, band]
    return pl.pallas_call(
        body,
        grid=(b, nq, nq),
        in_specs=in_specs,
        out_specs=pl.BlockSpec((1, t, w), lambda bi, qi, ki: (bi, qi, 0)),
        out_shape=jax.ShapeDtypeStruct((b, L, w), BF16),
        scratch_shapes=scratch,
        compiler_params=_cparams("arbitrary", "arbitrary", "arbitrary"),
        name=kind + "_attention",
    )(*args)


def _sortable(x):
    bits = pltpu.bitcast(x, jnp.int32)
    return bits ^ ((bits >> 31) & 0x7FFFFFFF)


def _index_kernel(qi_ref, w_ref, kie_ref, kio_ref, kiem_ref, kiom_ref, mask_ref, wb_ref, st_ref, smt_ref,
                  *, tc, w_lane0, n_meta, k_top):
    qt = pl.program_id(1)
    tq = qi_ref.shape[1]
    L = kie_ref.shape[1]
    n_ct = (qt + 1) * tq // tc
    npair = H_IDX // 2
    wscale = (H_IDX ** -0.5) * (D_IDX ** -0.5)

    wv = w_ref[0]
    for h in range(H_IDX):
        col = wv[:, w_lane0 + h:w_lane0 + h + 1] * wscale
        wb_ref[h] = jnp.broadcast_to(col, (tq, LANES))

    def score_tile(ke, ko):
        n = ke.shape[0]
        accs = [jnp.zeros((tq, LANES), F32) for _ in range(n // LANES)]
        for j in range(npair):
            lhs = qi_ref[0, :, j * LANES:(j + 1) * LANES]
            de = jnp.maximum(_dot_nt(lhs, ke), 0.0)
            do = jnp.maximum(_dot_nt(lhs, ko), 0.0)
            for c in range(n // LANES):
                cs = slice(c * LANES, (c + 1) * LANES)
                accs[c] = accs[c] + wb_ref[2 * j] * de[:, cs] + wb_ref[2 * j + 1] * do[:, cs]
        return accs[0] if len(accs) == 1 else jnp.concatenate(accs, axis=1)

    colm = lax.broadcasted_iota(jnp.int32, (tq, META_PAD), 1)
    sm = jnp.where(colm < n_meta, score_tile(kiem_ref[...], kiom_ref[...]), -jnp.inf)
    smt_ref[...] = _sortable(sm.T)

    qchunk = (qt * tq + lax.broadcasted_iota(jnp.int32, (tq, tc), 0)) // CHUNK

    def tile_body(c, carry):
        start = pl.multiple_of(c * tc, tc)
        sc = score_tile(kie_ref[0, pl.ds(start, tc), :], kio_ref[0, pl.ds(start, tc), :])
        kchunk = (start + lax.broadcasted_iota(jnp.int32, (tq, tc), 1)) // CHUNK
        sc = jnp.where(kchunk <= qchunk, sc, -jnp.inf)
        st_ref[c] = _sortable(sc.T)
        return carry

    lax.fori_loop(0, n_ct, tile_body, 0)

    sub = 8
    nacc = 4

    def count_tile(tile, thr8, accs):
        for r in range(tile.shape[0] // sub):
            accs[r % nacc] = accs[r % nacc] + (tile[r * sub:(r + 1) * sub, :] >= thr8).astype(jnp.int32)
        return accs

    def count_ge(thr8):
        accs = count_tile(smt_ref[...], thr8, [jnp.zeros((sub, tq), jnp.int32) for _ in range(nacc)])
        accs = lax.fori_loop(0, n_ct, lambda c, a: tuple(count_tile(st_ref[c], thr8, list(a))), tuple(accs))
        total = functools.reduce(lambda x, y: x + y, accs)
        return jnp.broadcast_to(jnp.sum(total, axis=0, keepdims=True), (sub, tq))

    def bit_body(it, ans_u):
        trial_u = ans_u | (jnp.int32(1) << (31 - it))
        cnt = count_ge(trial_u ^ INT_MIN)
        return jnp.where(cnt >= k_top, trial_u, ans_u)

    ans_u = lax.fori_loop(0, 32, bit_body, jnp.zeros((sub, tq), jnp.int32))
    thr = jnp.maximum(ans_u ^ INT_MIN, KEY_NEG_INF + 1)[:1]

    def mask_of(keys_t):
        return jnp.where(keys_t >= thr, 0.0, NEG).astype(F32).T.astype(mask_ref.dtype)

    mask_ref[0, :, L:] = mask_of(smt_ref[...])
    for c in range(L // tc):
        cols = slice(c * tc, (c + 1) * tc)

        @pl.when(c < n_ct)
        def _():
            mask_ref[0, :, cols] = mask_of(st_ref[c])

        @pl.when(c >= n_ct)
        def _():
            mask_ref[0, :, cols] = jnp.full((tq, tc), NEG, mask_ref.dtype)


def _indexer_mask(qi, slab, kie, kio, kiem, kiom, w_lane0, n_meta, k_top, tq=256, tc=256):
    b, L, _ = qi.shape
    tq = _tile(L, tq)
    tc = _tile(tq, tc)
    wq = H_IDX * D_IDX
    return pl.pallas_call(
        functools.partial(_index_kernel, tc=tc, w_lane0=w_lane0, n_meta=n_meta, k_top=k_top),
        grid=(b, L // tq),
        in_specs=[
            pl.BlockSpec((1, tq, wq), lambda bi, qi: (bi, qi, 0)),
            pl.BlockSpec((1, tq, LANES), lambda bi, qi: (bi, qi, 0)),
            pl.BlockSpec((1, L, LANES), lambda bi, qi: (bi, 0, 0)),
            pl.BlockSpec((1, L, LANES), lambda bi, qi: (bi, 0, 0)),
            pl.BlockSpec((META_PAD, LANES), lambda bi, qi: (0, 0)),
            pl.BlockSpec((META_PAD, LANES), lambda bi, qi: (0, 0)),
        ],
        out_specs=pl.BlockSpec((1, tq, L + META_PAD), lambda bi, qi: (bi, qi, 0)),
        out_shape=jax.ShapeDtypeStruct((b, L, L + META_PAD), BF16),
        scratch_shapes=[
            pltpu.VMEM((H_IDX, tq, LANES), F32),
            pltpu.VMEM((L // tc, tc, tq), jnp.int32),
            pltpu.VMEM((META_PAD, tq), jnp.int32),
        ],
        compiler_params=_cparams("arbitrary", "arbitrary"),
        name="indexer_topk_mask",
    )(qi, slab, kie, kio, kiem, kiom)


def _t5_bucket(rel, n_buckets):
    half = n_buckets // 2
    max_exact = half // 2
    ret = jnp.where(rel > 0, half, 0)
    n = jnp.abs(rel)
    nf = jnp.maximum(n, 1).astype(jnp.float32)
    large = max_exact + (jnp.log(nf / max_exact) / math.log(MAX_DISTANCE / max_exact)
                         * (half - max_exact)).astype(jnp.int32)
    large = jnp.minimum(large, half - 1)
    return ret + jnp.where(n < max_exact, n, large)


def _band_kernel(bucket_ref, far_ref, rb_ref, o_ref, *, n_buckets, nh):
    bucket = bucket_ref[0]
    far = far_ref[0]
    for h in range(nh):
        acc = jnp.zeros(bucket.shape, F32)
        for bk in range(n_buckets):
            acc = jnp.where(bucket == bk, rb_ref[bk, h], acc)
        o_ref[0, h] = (acc - rb_ref[far, h]) * LOG2E


def _band_tables(rel_bias, n_meta):
    n_buckets, nh = rel_bias.shape
    i = jnp.arange(LANES, dtype=jnp.int32)[:, None]
    j = jnp.arange(LANES, dtype=jnp.int32)[None, :]
    rel = jnp.stack([j - i, j - i - LANES, j - n_meta - i])
    bucket = _t5_bucket(rel, n_buckets).astype(jnp.int32)
    far = _t5_bucket(jnp.full((1,), -MAX_DISTANCE, jnp.int32), n_buckets).astype(jnp.int32)
    return pl.pallas_call(
        functools.partial(_band_kernel, n_buckets=n_buckets, nh=nh),
        grid=(3,),
        in_specs=[
            pl.BlockSpec((1, LANES, LANES), lambda t: (t, 0, 0)),
            pl.BlockSpec(memory_space=pltpu.SMEM),
            pl.BlockSpec(memory_space=pltpu.SMEM),
        ],
        out_specs=pl.BlockSpec((1, nh, LANES, LANES), lambda t: (t, 0, 0, 0)),
        out_shape=jax.ShapeDtypeStruct((3, nh, LANES, LANES), F32),
        compiler_params=_cparams("arbitrary"),
        name="rel_bias_band",
    )(bucket, far, rel_bias.astype(F32))


def kernel(x, meta_tokens, attn_norm_g, w_in, forget_bias, rel_bias, w_branch_fox, w_branch_dsa,
           w_out, mlp_norm_g, w_up, w_down, final_norm_g):
    b, L, d = x.shape
    n_meta = meta_tokens.shape[0]
    h_fox = forget_bias.shape[1]
    h_dsa = rel_bias.shape[1]
    w_fox, w_dsa = w_branch_fox.shape[1], w_branch_dsa.shape[1]
    hd = w_fox // h_fox
    w_idx = H_IDX * D_IDX
    assert attn_norm_g.shape[0] == 1, "single-layer block"
    assert w_dsa // h_dsa == hd and hd % LANES == 0
    assert n_meta <= META_PAD and L % CHUNK == 0 and L % META_PAD == 0
    assert MAX_DISTANCE <= LANES and CHUNK <= LANES
    k_top = min(TOPK_MAX, L // 4)

    off_fa = 3 * w_fox
    off_dsa = off_fa + h_fox
    off_qi = off_dsa + 3 * w_dsa
    off_ki = off_qi + w_idx
    off_gate = off_ki + D_IDX + H_IDX
    assert off_gate + 2 * d == w_in.shape[2] and D_IDX + H_IDX + h_fox <= LANES
    wt = jnp.swapaxes(w_in, 1, 2)[0]
    w_fox3 = _columns_bf16(wt, [(0, 3 * w_fox)])
    w_dsa3 = _columns_bf16(wt, [(off_dsa, 3 * w_dsa)])
    w_qi = _columns_bf16(wt, [(off_qi, w_idx)])
    w_gate = _columns_bf16(wt, [(off_gate, 2 * d)])
    w_small = _columns_bf16(wt, [(off_ki, D_IDX + H_IDX), (off_fa, h_fox)])
    wi_lane0, fa_lane0 = D_IDX, D_IDX + H_IDX

    x2 = x.reshape(b * L, d)
    meta = jnp.zeros((META_PAD, d), x.dtype).at[:n_meta].set(meta_tokens.astype(x.dtype))
    u = _rmsnorm(x2, attn_norm_g[0], BF16)
    um = _rmsnorm(meta, attn_norm_g[0], BF16)
    fox3 = _mm(u, w_fox3, BF16).reshape(b, L, -1)
    fox3m = _mm(um, w_fox3, BF16)
    dsa3 = _mm(u, w_dsa3, BF16).reshape(b, L, -1)
    dsa3m = _mm(um, w_dsa3, BF16)
    qi = _mm(u, w_qi, BF16).reshape(b, L, -1)
    gate = _mm(u, w_gate, F32, act="sigmoid")
    slab = _mm(u, w_small, F32, tn=LANES).reshape(b, L, LANES)
    slabm = _mm(um, w_small, F32, tn=LANES)

    bias_row = jnp.zeros((1, LANES), F32).at[0, fa_lane0:fa_lane0 + h_fox].set(forget_bias[0].astype(F32))
    dt, dtm = _decay(slab, slabm, bias_row, fa_lane0, h_fox, n_meta)
    o_fox = _attention("fox", fox3, fox3m, h_fox, hd, 512, (dt, dtm, n_meta))

    zpad = jnp.zeros(slab.shape[:-1] + (LANES - D_IDX,), BF16)
    ki = slab[..., :D_IDX].astype(BF16)
    kie, kio = jnp.concatenate([ki, zpad], axis=-1), jnp.concatenate([zpad, ki], axis=-1)
    kim = slabm[:, :D_IDX].astype(BF16)
    kiem = jnp.concatenate([kim, zpad[0, :META_PAD]], axis=-1)
    kiom = jnp.concatenate([zpad[0, :META_PAD], kim], axis=-1)
    mask = _indexer_mask(qi, slab, kie, kio, kiem, kiom, wi_lane0, n_meta, k_top)
    band = _band_tables(rel_bias, n_meta)
    o_dsa = _attention("dsa", dsa3, dsa3m, h_dsa, hd, 512, (mask, band))

    mixed = _branch_merge(o_fox.reshape(b * L, w_fox), o_dsa.reshape(b * L, w_dsa),
                          w_branch_fox[0].astype(BF16), w_branch_dsa[0].astype(BF16), gate, BF16)
    h2 = _mm(mixed, w_out[0].astype(BF16), F32, res=x2)

    u2 = _rmsnorm(h2, mlp_norm_g[0], BF16)
    a = _mm(u2, w_up[0].astype(BF16), BF16, act="relu2")
    h3 = _mm_acc(a, w_down[0].astype(BF16), h2, F32)
    return _rmsnorm(h3, final_norm_g, x.dtype).reshape(b, L, d)
```

```python
import functools
import math

import jax
import jax.numpy as jnp
from jax import lax
from jax.experimental import pallas as pl
from jax.experimental.pallas import tpu as pltpu

F32 = jnp.float32
BF16 = jnp.bfloat16

CHUNK = 64
H_IDX = 32
D_IDX = 64
TOPK_MAX = 256
MAX_DISTANCE = 128
RMS_EPS = 1e-6

LANES = 128
SUBLANES = 8
BF16_SUBLANES = 16
META_PAD = 128
NEG = -1e30
LOG2E = math.log2(math.e)
INT_MIN = -(2 ** 31)
KEY_NEG_INF = -2139095041
VMEM_LIMIT = 56 * 1024 * 1024
VMEM_LIMIT_BIG = 60 * 1024 * 1024


def _tile(dim, pref):
    if dim <= pref:
        return dim
    for t in range(pref - pref % LANES, 0, -LANES):
        if dim % t == 0:
            return t
    raise ValueError((dim, pref))


def _cparams(*sem, vmem=VMEM_LIMIT):
    return pltpu.CompilerParams(dimension_semantics=sem, vmem_limit_bytes=vmem)


def _rmsnorm_kernel(x_ref, g_ref, o_ref):
    x = x_ref[...].astype(F32)
    ms = jnp.mean(x * x, axis=-1, keepdims=True)
    y = x * lax.rsqrt(ms + RMS_EPS)
    o_ref[...] = (y * g_ref[...]).astype(o_ref.dtype)


def _rmsnorm(x, g, out_dtype):
    m, d = x.shape
    tr = _tile(m, 256)
    return pl.pallas_call(
        _rmsnorm_kernel,
        grid=(m // tr,),
        in_specs=[pl.BlockSpec((tr, d), lambda i: (i, 0)), pl.BlockSpec((1, d), lambda i: (0, 0))],
        out_specs=pl.BlockSpec((tr, d), lambda i: (i, 0)),
        out_shape=jax.ShapeDtypeStruct((m, d), out_dtype),
        compiler_params=_cparams("arbitrary"),
        name="rmsnorm",
    )(x, g.reshape(1, d).astype(F32))


def _columns_kernel(*refs):
    *w_refs, o_ref = refs
    parts = [r[...] for r in w_refs]
    pad = o_ref.shape[1] - sum(p.shape[0] for p in parts)
    if pad:
        parts.append(jnp.zeros((pad, parts[0].shape[1]), parts[0].dtype))
    rows = parts[0] if len(parts) == 1 else jnp.concatenate(parts, axis=0)
    o_ref[...] = rows.T.astype(o_ref.dtype)


def _columns_bf16(wt, ranges, chunk=512):
    n, k = wt.shape
    if len(ranges) == 1 and ranges[0][1] > chunk:
        (lo, width), = ranges
        chunk = _tile(width, chunk)
        steps, widths, out_w = width // chunk, [chunk], chunk
        offs = [lo]
    else:
        steps, widths = 1, [w for _, w in ranges]
        offs = [lo for lo, _ in ranges]
        out_w = -(-sum(widths) // LANES) * LANES
    assert all(o % SUBLANES == 0 and w % SUBLANES == 0 for o, w in zip(offs, widths))
    in_specs = [pl.BlockSpec((pl.Element(w), pl.Element(k)),
                             functools.partial(lambda i, o, w: (pl.multiple_of(o + i * w, SUBLANES), 0), o=o, w=w))
                for o, w in zip(offs, widths)]
    return pl.pallas_call(
        _columns_kernel,
        grid=(steps,),
        in_specs=in_specs,
        out_specs=pl.BlockSpec((k, out_w), lambda i: (0, i)),
        out_shape=jax.ShapeDtypeStruct((k, out_w * steps), BF16),
        compiler_params=_cparams("arbitrary"),
        name="weight_columns",
    )(*([wt] * len(offs)))


def _act(acc, act):
    if act == "sigmoid":
        return jax.nn.sigmoid(acc)
    if act == "relu2":
        r = jnp.maximum(acc, 0.0)
        return r * r
    return acc


def _mm_kernel(*refs, act, has_res, has_narrow, n_cast):
    refs = list(refs)
    a_ref, b_ref = refs[:2]
    pos = 2
    r_ref = refs[pos] if has_res else None
    pos += has_res
    bn_ref = refs[pos] if has_narrow else None
    pos += has_narrow
    cast_in = refs[pos:pos + n_cast]
    pos += n_cast
    o_ref = refs[pos]
    pos += 1
    on_ref = refs[pos] if has_narrow else None
    pos += has_narrow
    cast_out = refs[pos:pos + n_cast]

    acc = _act(jnp.dot(a_ref[...], b_ref[...], preferred_element_type=F32), act)
    if has_res:
        acc = r_ref[...] + acc
    o_ref[...] = acc.astype(o_ref.dtype)
    if has_narrow:
        @pl.when(pl.program_id(1) == 0)
        def _():
            on_ref[...] = jnp.dot(a_ref[...], bn_ref[...], preferred_element_type=F32).astype(on_ref.dtype)
    for ci, co in zip(cast_in, cast_out):
        co[...] = ci[...].astype(co.dtype)


def _mm(a, b, out_dtype, act=None, res=None, tm=1024, tn=512, narrow=None, casts=()):
    m, k = a.shape
    _, n = b.shape
    tm, tn = _tile(m, tm), _tile(n, tn)
    ni, nj = m // tm, n // tn
    in_specs = [pl.BlockSpec((tm, k), lambda i, j: (i, 0)), pl.BlockSpec((k, tn), lambda i, j: (0, j))]
    args = [a, b]
    out_specs = [pl.BlockSpec((tm, tn), lambda i, j: (i, j))]
    out_shape = [jax.ShapeDtypeStruct((m, n), out_dtype)]
    if res is not None:
        in_specs.append(pl.BlockSpec((tm, tn), lambda i, j: (i, j)))
        args.append(res)
    if narrow is not None:
        in_specs.append(pl.BlockSpec((k, LANES), lambda i, j: (0, 0)))
        args.append(narrow)
        out_specs.append(pl.BlockSpec((tm, LANES), lambda i, j: (i, 0)))
        out_shape.append(jax.ShapeDtypeStruct((m, LANES), F32))
    riding = []
    for c in casts:
        rows, cols = c.shape
        slab = rows // (ni * nj)
        riding.append(slab * ni * nj == rows and slab % BF16_SUBLANES == 0)
        if riding[-1]:
            spec = pl.BlockSpec((slab, cols), lambda i, j: (i * nj + j, 0))
            in_specs.append(spec)
            args.append(c)
            out_specs.append(spec)
            out_shape.append(jax.ShapeDtypeStruct((rows, cols), BF16))
    outs = list(pl.pallas_call(
        functools.partial(_mm_kernel, act=act, has_res=res is not None, has_narrow=narrow is not None,
                          n_cast=sum(riding)),
        grid=(ni, nj),
        in_specs=in_specs,
        out_specs=out_specs,
        out_shape=out_shape,
        compiler_params=_cparams("arbitrary", "arbitrary"),
        name="matmul",
    )(*args))
    n_main = len(outs) - sum(riding)
    cast_outs = iter(outs[n_main:])
    outs = outs[:n_main] + [next(cast_outs) if r else c.astype(BF16) for r, c in zip(riding, casts)]
    return outs[0] if len(outs) == 1 else tuple(outs)


def _mm_acc_kernel(a_ref, b_ref, r_ref, o_ref):
    kk = pl.program_id(2)
    part = jnp.dot(a_ref[...], b_ref[...], preferred_element_type=F32)

    @pl.when(kk == 0)
    def _():
        o_ref[...] = r_ref[...] + part

    @pl.when(kk > 0)
    def _():
        o_ref[...] += part


def _mm_acc(a, b, res, out_dtype, tm=1024, tn=1024, tk=4096):
    assert out_dtype == F32 and res.dtype == F32
    m, k = a.shape
    _, n = b.shape
    tm, tn, tk = _tile(m, tm), _tile(n, tn), _tile(k, tk)
    return pl.pallas_call(
        _mm_acc_kernel,
        grid=(m // tm, n // tn, k // tk),
        in_specs=[
            pl.BlockSpec((tm, tk), lambda i, j, kk: (i, kk)),
            pl.BlockSpec((tk, tn), lambda i, j, kk: (kk, j)),
            pl.BlockSpec((tm, tn), lambda i, j, kk: (i, j)),
        ],
        out_specs=pl.BlockSpec((tm, tn), lambda i, j, kk: (i, j)),
        out_shape=jax.ShapeDtypeStruct((m, n), out_dtype),
        compiler_params=_cparams("arbitrary", "arbitrary", "arbitrary", vmem=VMEM_LIMIT_BIG),
        name="matmul_acc",
    )(a, b, res)


def _branch_kernel(of_ref, od_ref, wf_ref, wd_ref, g0_ref, g1_ref, o_ref):
    yf = jnp.dot(of_ref[...], wf_ref[...], preferred_element_type=F32)
    yd = jnp.dot(od_ref[...], wd_ref[...], preferred_element_type=F32)
    o_ref[...] = (g0_ref[...] * yf + g1_ref[...] * yd).astype(o_ref.dtype)


def _branch_merge(o_fox, o_dsa, wf, wd, gate, out_dtype, tm=512, tn=1024):
    m, kf = o_fox.shape
    _, kd = o_dsa.shape
    n = wf.shape[1]
    tm, tn = _tile(m, tm), _tile(n, tn)
    nj = n // tn
    return pl.pallas_call(
        _branch_kernel,
        grid=(m // tm, nj),
        in_specs=[
            pl.BlockSpec((tm, kf), lambda i, j: (i, 0)),
            pl.BlockSpec((tm, kd), lambda i, j: (i, 0)),
            pl.BlockSpec((kf, tn), lambda i, j: (0, j)),
            pl.BlockSpec((kd, tn), lambda i, j: (0, j)),
            pl.BlockSpec((tm, tn), lambda i, j: (i, j)),
            pl.BlockSpec((tm, tn), lambda i, j: (i, j + nj)),
        ],
        out_specs=pl.BlockSpec((tm, tn), lambda i, j: (i, j)),
        out_shape=jax.ShapeDtypeStruct((m, n), out_dtype),
        compiler_params=_cparams("arbitrary", "arbitrary"),
        name="branch_merge",
    )(o_fox, o_dsa, wf, wd, gate, gate)


def _split3(x):
    hi = x.astype(BF16)
    r1 = x - hi.astype(F32)
    mid = r1.astype(BF16)
    lo = (r1 - mid.astype(F32)).astype(BF16)
    return hi, mid, lo


def _cumsum_lanes(xt, upper):
    out = None
    for part in _split3(xt):
        d = jnp.dot(part, upper, preferred_element_type=F32)
        out = d if out is None else out + d
    return out


def _decay_kernel(slab_ref, slabm_ref, bias_ref, dt_ref, dtm_ref, *, blk, lane0, nh, n_meta):
    L = slab_ref.shape[1]
    bias = bias_ref[...]

    def upper(n):
        r = lax.broadcasted_iota(jnp.int32, (n, n), 0)
        c = lax.broadcasted_iota(jnp.int32, (n, n), 1)
        return (r <= c).astype(BF16)

    xm = jax.nn.log_sigmoid(slabm_ref[...] + bias)
    dm = _cumsum_lanes(xm.T, upper(META_PAD))
    dtm_ref[0] = dm[lane0:lane0 + nh, :] * -LOG2E
    carry = dm[:, n_meta - 1:n_meta]
    up = upper(blk)
    for i in range(L // blk):
        x = jax.nn.log_sigmoid(slab_ref[0, i * blk:(i + 1) * blk, :] + bias)
        d = _cumsum_lanes(x.T, up) + carry
        dt_ref[0, :, i * blk:(i + 1) * blk] = d[lane0:lane0 + nh, :] * -LOG2E
        carry = d[:, blk - 1:blk]


def _decay(slab, slabm, bias_row, lane0, nh, n_meta):
    b, L, _ = slab.shape
    blk = _tile(L, 256)
    return pl.pallas_call(
        functools.partial(_decay_kernel, blk=blk, lane0=lane0, nh=nh, n_meta=n_meta),
        grid=(b,),
        in_specs=[
            pl.BlockSpec((1, L, LANES), lambda i: (i, 0, 0)),
            pl.BlockSpec((META_PAD, LANES), lambda i: (0, 0)),
            pl.BlockSpec((1, LANES), lambda i: (0, 0)),
        ],
        out_specs=[
            pl.BlockSpec((1, nh, L), lambda i: (i, 0, 0)),
            pl.BlockSpec((1, nh, META_PAD), lambda i: (i, 0, 0)),
        ],
        out_shape=[
            jax.ShapeDtypeStruct((b, nh, L), F32),
            jax.ShapeDtypeStruct((b, nh, META_PAD), F32),
        ],
        compiler_params=_cparams("arbitrary"),
        name="forget_cumsum",
    )(slab, slabm, bias_row)


def _dot_nt(a, b):
    return lax.dot_general(a, b, (((1,), (1,)), ((), ())), preferred_element_type=F32)


def _online_softmax_step(h, hd, s2, v, m_ref, l_ref, acc_ref):
    ts = s2.shape[1]
    m_prev = m_ref[h]
    m_new = jnp.maximum(m_prev, jnp.max(s2, axis=1, keepdims=True))
    alpha = jnp.exp2(m_prev - m_new)
    p = jnp.exp2(s2 - jnp.tile(m_new, (1, ts // LANES))).astype(BF16)
    v_ones = jnp.concatenate([v, jnp.ones((ts, LANES), v.dtype)], axis=1)
    pv = jnp.dot(p, v_ones, preferred_element_type=F32)
    hs = slice(h * hd, (h + 1) * hd)
    acc_ref[:, hs] = acc_ref[:, hs] * jnp.tile(alpha, (1, hd // LANES)) + pv[:, :hd]
    l_ref[h] = alpha * l_ref[h] + pv[:, hd:]
    m_ref[h] = m_new


def _attn_init(m_ref, l_ref, acc_ref):
    m_ref[...] = jnp.full_like(m_ref, NEG)
    l_ref[...] = jnp.zeros_like(l_ref)
    acc_ref[...] = jnp.zeros_like(acc_ref)


def _attn_finish(o_ref, l_ref, acc_ref, nh, hd):
    for h in range(nh):
        hs = slice(h * hd, (h + 1) * hd)
        o_ref[0, :, hs] = (acc_ref[:, hs] / jnp.tile(l_ref[h], (1, hd // LANES))).astype(o_ref.dtype)


def _fox_kernel(q_ref, k_ref, v_ref, km_ref, vm_ref, dt_ref, dtm_ref, o_ref, m_ref, l_ref, acc_ref,
                *, nh, hd, n_meta):
    qt, kt = pl.program_id(1), pl.program_id(2)
    tq, ts = q_ref.shape[1], k_ref.shape[1]
    scale = hd ** -0.5 * LOG2E

    @pl.when(kt == 0)
    def _():
        _attn_init(m_ref, l_ref, acc_ref)

    def frames(diag):
        if diag:
            row = lax.broadcasted_iota(jnp.int32, (tq, ts), 0)
            col = lax.broadcasted_iota(jnp.int32, (tq, ts), 1)
            causal = col <= row
            colm = lax.broadcasted_iota(jnp.int32, (tq, META_PAD), 1)
            pad = jnp.where(colm < n_meta, 0.0, NEG).astype(F32)
        for h in range(nh):
            hs = slice(h * hd, (h + 1) * hd)
            q = q_ref[0, :, hs]
            s = _dot_nt(q, k_ref[0, :, hs]) * scale + dt_ref[0, h:h + 1, :]
            v = v_ref[0, :, hs]
            if diag:
                sm = _dot_nt(q, km_ref[:, hs]) * scale + dtm_ref[0, h:h + 1, :] + pad
                s = jnp.concatenate([jnp.where(causal, s, NEG), sm], axis=1)
                v = jnp.concatenate([v, vm_ref[:, hs]], axis=0)
            _online_softmax_step(h, hd, s, v, m_ref, l_ref, acc_ref)

    @pl.when(kt < qt)
    def _():
        frames(False)

    @pl.when(kt == qt)
    def _():
        frames(True)
        _attn_finish(o_ref, l_ref, acc_ref, nh, hd)


def _add_blocks(s, adds):
    rows = []
    for ib in range(s.shape[0] // LANES):
        r = s[ib * LANES:(ib + 1) * LANES]
        if any(i == ib for i, _ in adds):
            r = jnp.concatenate(
                [r[:, jb * LANES:(jb + 1) * LANES] + adds[(ib, jb)] if (ib, jb) in adds
                 else r[:, jb * LANES:(jb + 1) * LANES] for jb in range(s.shape[1] // LANES)], axis=1)
        rows.append(r)
    return jnp.concatenate(rows, axis=0)


def _dsa_kernel(q_ref, k_ref, v_ref, km_ref, vm_ref, mask_ref, maskm_ref, band_ref, o_ref,
                m_ref, l_ref, acc_ref, *, nh, hd):
    qt, kt = pl.program_id(1), pl.program_id(2)
    tq, ts = q_ref.shape[1], k_ref.shape[1]
    nb = tq // LANES
    scale = hd ** -0.5 * LOG2E

    @pl.when(kt == 0)
    def _():
        _attn_init(m_ref, l_ref, acc_ref)

    def frames(kind):
        maskf = mask_ref[0].astype(F32)
        if kind == "diag":
            maskm = maskm_ref[0].astype(F32)
            first = (qt == 0).astype(F32)
        for h in range(nh):
            hs = slice(h * hd, (h + 1) * hd)
            q = q_ref[0, :, hs]
            s = _dot_nt(q, k_ref[0, :, hs]) * scale + maskf
            v = v_ref[0, :, hs]
            if kind == "near":
                s = _add_blocks(s, {(0, ts // LANES - 1): band_ref[1, h]})
            elif kind == "diag":
                sm = _dot_nt(q, km_ref[:, hs]) * scale + maskm
                s = jnp.concatenate([s, sm], axis=1)
                adds = {(0, ts // LANES): band_ref[2, h] * first}
                for ib in range(nb):
                    adds[(ib, ib)] = band_ref[0, h]
                    if ib > 0:
                        adds[(ib, ib - 1)] = band_ref[1, h]
                s = _add_blocks(s, adds)
                v = jnp.concatenate([v, vm_ref[:, hs]], axis=0)
            _online_softmax_step(h, hd, s, v, m_ref, l_ref, acc_ref)

    @pl.when(kt < qt - 1)
    def _():
        frames("far")

    @pl.when(kt == qt - 1)
    def _():
        frames("near")

    @pl.when(kt == qt)
    def _():
        frames("diag")
        _attn_finish(o_ref, l_ref, acc_ref, nh, hd)


def _attention(kind, big, bigm, nh, hd, tile, extra):
    b, L, _ = big.shape
    w = nh * hd
    q_blk, k_blk, v_blk = 0, 1, 2
    t = _tile(L, tile)
    nq = L // t
    q_spec = pl.BlockSpec((1, t, w), lambda bi, qi, ki: (bi, qi, q_blk))
    kv = lambda blk: pl.BlockSpec((1, t, w), lambda bi, qi, ki: (bi, jnp.minimum(ki, qi), blk))
    kvm = lambda blk: pl.BlockSpec((META_PAD, w), lambda bi, qi, ki: (0, blk))
    scratch = [
        pltpu.VMEM((nh, t, LANES), F32),
        pltpu.VMEM((nh, t, LANES), F32),
        pltpu.VMEM((t, w), F32),
    ]
    if kind == "fox":
        dt, dtm, n_meta = extra
        body = functools.partial(_fox_kernel, nh=nh, hd=hd, n_meta=n_meta)
        in_specs = [q_spec, kv(k_blk), kv(v_blk), kvm(k_blk), kvm(v_blk),
                    pl.BlockSpec((1, nh, t), lambda bi, qi, ki: (bi, 0, jnp.minimum(ki, qi))),
                    pl.BlockSpec((1, nh, META_PAD), lambda bi, qi, ki: (bi, 0, 0))]
        args = [big, big, big, bigm, bigm, dt, dtm]
    else:
        mask, band = extra
        body = functools.partial(_dsa_kernel, nh=nh, hd=hd)
        in_specs = [q_spec, kv(k_blk), kv(v_blk), kvm(k_blk), kvm(v_blk),
                    pl.BlockSpec((1, t, t), lambda bi, qi, ki: (bi, qi, jnp.minimum(ki, qi))),
                    pl.BlockSpec((1, t, META_PAD), lambda bi, qi, ki: (bi, qi, L // META_PAD)),
                    pl.BlockSpec((3, nh, LANES, LANES), lambda bi, qi, ki: (0, 0, 0, 0))]
        args = [big, big, big, bigm, bigm, mask, mask, band]
    return pl.pallas_call(
        body,
        grid=(b, nq, nq),
        in_specs=in_specs,
        out_specs=pl.BlockSpec((1, t, w), lambda bi, qi, ki: (bi, qi, 0)),
        out_shape=jax.ShapeDtypeStruct((b, L, w), BF16),
        scratch_shapes=scratch,
        compiler_params=_cparams("arbitrary", "arbitrary", "arbitrary"),
        name=kind + "_attention",
    )(*args)


def _sortable(x):
    bits = pltpu.bitcast(x, jnp.int32)
    return bits ^ ((bits >> 31) & 0x7FFFFFFF)


def _index_kernel(qi_ref, w_ref, kie_ref, kio_ref, kiem_ref, kiom_ref, mask_ref, wb_ref, st_ref, smt_ref,
                  *, tc, w_lane0, n_meta, k_top):
    qt = pl.program_id(1)
    tq = qi_ref.shape[1]
    L = kie_ref.shape[1]
    n_ct = (qt + 1) * tq // tc
    npair = H_IDX // 2
    wscale = (H_IDX ** -0.5) * (D_IDX ** -0.5)

    wv = w_ref[0]
    for h in range(H_IDX):
        col = wv[:, w_lane0 + h:w_lane0 + h + 1] * wscale
        wb_ref[h] = jnp.broadcast_to(col, (tq, LANES))

    def score_tile(ke, ko):
        n = ke.shape[0]
        accs = [jnp.zeros((tq, LANES), F32) for _ in range(n // LANES)]
        for j in range(npair):
            lhs = qi_ref[0, :, j * LANES:(j + 1) * LANES]
            de = jnp.maximum(_dot_nt(lhs, ke), 0.0)
            do = jnp.maximum(_dot_nt(lhs, ko), 0.0)
            for c in range(n // LANES):
                cs = slice(c * LANES, (c + 1) * LANES)
                accs[c] = accs[c] + wb_ref[2 * j] * de[:, cs] + wb_ref[2 * j + 1] * do[:, cs]
        return accs[0] if len(accs) == 1 else jnp.concatenate(accs, axis=1)

    colm = lax.broadcasted_iota(jnp.int32, (tq, META_PAD), 1)
    sm = jnp.where(colm < n_meta, score_tile(kiem_ref[...], kiom_ref[...]), -jnp.inf)
    smt_ref[...] = _sortable(sm.T)

    qchunk = (qt * tq + lax.broadcasted_iota(jnp.int32, (tq, tc), 0)) // CHUNK

    def tile_body(c, carry):
        start = pl.multiple_of(c * tc, tc)
        sc = score_tile(kie_ref[0, pl.ds(start, tc), :], kio_ref[0, pl.ds(start, tc), :])
        kchunk = (start + lax.broadcasted_iota(jnp.int32, (tq, tc), 1)) // CHUNK
        sc = jnp.where(kchunk <= qchunk, sc, -jnp.inf)
        st_ref[c] = _sortable(sc.T)
        return carry

    lax.fori_loop(0, n_ct, tile_body, 0)

    sub = 8
    nacc = 4

    def count_tile(tile, thr8, accs):
        for r in range(tile.shape[0] // sub):
            accs[r % nacc] = accs[r % nacc] + (tile[r * sub:(r + 1) * sub, :] >= thr8).astype(jnp.int32)
        return accs

    def count_ge(thr8):
        accs = count_tile(smt_ref[...], thr8, [jnp.zeros((sub, tq), jnp.int32) for _ in range(nacc)])
        accs = lax.fori_loop(0, n_ct, lambda c, a: tuple(count_tile(st_ref[c], thr8, list(a))), tuple(accs))
        total = functools.reduce(lambda x, y: x + y, accs)
        return jnp.broadcast_to(jnp.sum(total, axis=0, keepdims=True), (sub, tq))

    def bit_body(it, ans_u):
        trial_u = ans_u | (jnp.int32(1) << (31 - it))
        cnt = count_ge(trial_u ^ INT_MIN)
        return jnp.where(cnt >= k_top, trial_u, ans_u)

    ans_u = lax.fori_loop(0, 32, bit_body, jnp.zeros((sub, tq), jnp.int32))
    thr8 = jnp.maximum(ans_u ^ INT_MIN, KEY_NEG_INF + 1)
    thr = thr8[:1]

    cnt_ge = count_ge(thr8)
    tied = jnp.max(((cnt_ge > k_top) & (thr8 > KEY_NEG_INF + 1)).astype(jnp.int32)) > 0

    @pl.when(tied)
    def _():
        need = k_top - count_ge(thr8 + 1)[:1]
        posm = lax.broadcasted_iota(jnp.int32, (META_PAD, tq), 0)
        posf = n_meta + lax.broadcasted_iota(jnp.int32, (tc, tq), 0)

        def tied_before(cut):
            acc = jnp.sum(((smt_ref[...] == thr) & (posm < cut)).astype(jnp.int32), axis=0, keepdims=True)

            def body(c, acc):
                hit = (st_ref[c] == thr) & (posf + c * tc < cut)
                return acc + jnp.sum(hit.astype(jnp.int32), axis=0, keepdims=True)

            return lax.fori_loop(0, n_ct, body, acc)

        nbits = (L + META_PAD).bit_length()

        def pos_body(it, cut):
            trial = cut | (jnp.int32(1) << (nbits - 1 - it))
            return jnp.where(tied_before(trial) < need, trial, cut)

        cut = lax.fori_loop(0, nbits, pos_body, jnp.zeros((1, tq), jnp.int32))
        smt_ref[...] = smt_ref[...] - ((smt_ref[...] == thr) & (posm > cut)).astype(jnp.int32)

        def lower(c, carry):
            st_ref[c] = st_ref[c] - ((st_ref[c] == thr) & (posf + c * tc > cut)).astype(jnp.int32)
            return carry

        lax.fori_loop(0, n_ct, lower, 0)

    def mask_of(keys_t):
        return jnp.where(keys_t >= thr, 0.0, NEG).astype(F32).T.astype(mask_ref.dtype)

    mask_ref[0, :, L:] = mask_of(smt_ref[...])
    for c in range(L // tc):
        cols = slice(c * tc, (c + 1) * tc)

        @pl.when(c < n_ct)
        def _():
            mask_ref[0, :, cols] = mask_of(st_ref[c])

        @pl.when(c >= n_ct)
        def _():
            mask_ref[0, :, cols] = jnp.full((tq, tc), NEG, mask_ref.dtype)


def _indexer_mask(qi, slab, kie, kio, kiem, kiom, w_lane0, n_meta, k_top, tq=256, tc=256):
    b, L, _ = qi.shape
    tq = _tile(L, tq)
    tc = _tile(tq, tc)
    wq = H_IDX * D_IDX
    return pl.pallas_call(
        functools.partial(_index_kernel, tc=tc, w_lane0=w_lane0, n_meta=n_meta, k_top=k_top),
        grid=(b, L // tq),
        in_specs=[
            pl.BlockSpec((1, tq, wq), lambda bi, qi: (bi, qi, 0)),
            pl.BlockSpec((1, tq, LANES), lambda bi, qi: (bi, qi, 0)),
            pl.BlockSpec((1, L, LANES), lambda bi, qi: (bi, 0, 0)),
            pl.BlockSpec((1, L, LANES), lambda bi, qi: (bi, 0, 0)),
            pl.BlockSpec((META_PAD, LANES), lambda bi, qi: (0, 0)),
            pl.BlockSpec((META_PAD, LANES), lambda bi, qi: (0, 0)),
        ],
        out_specs=pl.BlockSpec((1, tq, L + META_PAD), lambda bi, qi: (bi, qi, 0)),
        out_shape=jax.ShapeDtypeStruct((b, L, L + META_PAD), BF16),
        scratch_shapes=[
            pltpu.VMEM((H_IDX, tq, LANES), F32),
            pltpu.VMEM((L // tc, tc, tq), jnp.int32),
            pltpu.VMEM((META_PAD, tq), jnp.int32),
        ],
        compiler_params=_cparams("arbitrary", "arbitrary"),
        name="indexer_topk_mask",
    )(qi, slab, kie, kio, kiem, kiom)


def _t5_bucket(rel, n_buckets):
    half = n_buckets // 2
    max_exact = half // 2
    ret = jnp.where(rel > 0, half, 0)
    n = jnp.abs(rel)
    nf = jnp.maximum(n, 1).astype(jnp.float32)
    large = max_exact + (jnp.log(nf / max_exact) / math.log(MAX_DISTANCE / max_exact)
                         * (half - max_exact)).astype(jnp.int32)
    large = jnp.minimum(large, half - 1)
    return ret + jnp.where(n < max_exact, n, large)


def _band_kernel(bucket_ref, far_ref, rb_ref, o_ref, *, n_buckets, nh):
    bucket = bucket_ref[0]
    far = far_ref[0]
    for h in range(nh):
        acc = jnp.zeros(bucket.shape, F32)
        for bk in range(n_buckets):
            acc = jnp.where(bucket == bk, rb_ref[bk, h], acc)
        o_ref[0, h] = (acc - rb_ref[far, h]) * LOG2E


def _band_tables(rel_bias, n_meta):
    n_buckets, nh = rel_bias.shape
    i = jnp.arange(LANES, dtype=jnp.int32)[:, None]
    j = jnp.arange(LANES, dtype=jnp.int32)[None, :]
    rel = jnp.stack([j - i, j - i - LANES, j - n_meta - i])
    bucket = _t5_bucket(rel, n_buckets).astype(jnp.int32)
    far = _t5_bucket(jnp.full((1,), -MAX_DISTANCE, jnp.int32), n_buckets).astype(jnp.int32)
    return pl.pallas_call(
        functools.partial(_band_kernel, n_buckets=n_buckets, nh=nh),
        grid=(3,),
        in_specs=[
            pl.BlockSpec((1, LANES, LANES), lambda t: (t, 0, 0)),
            pl.BlockSpec(memory_space=pltpu.SMEM),
            pl.BlockSpec(memory_space=pltpu.SMEM),
        ],
        out_specs=pl.BlockSpec((1, nh, LANES, LANES), lambda t: (t, 0, 0, 0)),
        out_shape=jax.ShapeDtypeStruct((3, nh, LANES, LANES), F32),
        compiler_params=_cparams("arbitrary"),
        name="rel_bias_band",
    )(bucket, far, rel_bias.astype(F32))


def kernel(x, meta_tokens, attn_norm_g, w_in, forget_bias, rel_bias, w_branch_fox, w_branch_dsa,
           w_out, mlp_norm_g, w_up, w_down, final_norm_g):
    b, L, d = x.shape
    n_meta = meta_tokens.shape[0]
    h_fox = forget_bias.shape[1]
    h_dsa = rel_bias.shape[1]
    w_fox, w_dsa = w_branch_fox.shape[1], w_branch_dsa.shape[1]
    hd = w_fox // h_fox
    w_idx = H_IDX * D_IDX
    assert attn_norm_g.shape[0] == 1, "single-layer block"
    assert w_dsa // h_dsa == hd and hd % LANES == 0
    assert n_meta <= META_PAD and L % CHUNK == 0 and L % META_PAD == 0
    assert MAX_DISTANCE <= LANES and CHUNK <= LANES
    k_top = min(TOPK_MAX, L // 4)

    off_fa = 3 * w_fox
    off_dsa = off_fa + h_fox
    off_qi = off_dsa + 3 * w_dsa
    off_ki = off_qi + w_idx
    off_gate = off_ki + D_IDX + H_IDX
    assert off_gate + 2 * d == w_in.shape[2] and D_IDX + H_IDX + h_fox <= LANES
    wt = jnp.swapaxes(w_in, 1, 2)[0]
    w_fox3 = _columns_bf16(wt, [(0, 3 * w_fox)])
    w_dsa3 = _columns_bf16(wt, [(off_dsa, 3 * w_dsa)])
    w_qi = _columns_bf16(wt, [(off_qi, w_idx)])
    w_gate = _columns_bf16(wt, [(off_gate, 2 * d)])
    w_small = _columns_bf16(wt, [(off_ki, D_IDX + H_IDX), (off_fa, h_fox)])
    wi_lane0, fa_lane0 = D_IDX, D_IDX + H_IDX

    x2 = x.reshape(b * L, d)
    meta = jnp.zeros((META_PAD, d), x.dtype).at[:n_meta].set(meta_tokens.astype(x.dtype))
    u = _rmsnorm(x2, attn_norm_g[0], BF16)
    um = _rmsnorm(meta, attn_norm_g[0], BF16)
    fox3 = _mm(u, w_fox3, BF16).reshape(b, L, -1)
    fox3m = _mm(um, w_fox3, BF16)
    dsa3 = _mm(u, w_dsa3, BF16).reshape(b, L, -1)
    dsa3m = _mm(um, w_dsa3, BF16)
    qi, slab, w_out_b, w_bf_b, w_bd_b = _mm(u, w_qi, BF16, narrow=w_small,
                                            casts=(w_out[0], w_branch_fox[0], w_branch_dsa[0]))
    qi, slab = qi.reshape(b, L, -1), slab.reshape(b, L, LANES)
    gate, w_up_b = _mm(u, w_gate, F32, act="sigmoid", casts=(w_up[0],))
    slabm = _mm(um, w_small, F32, tn=LANES)

    bias_row = jnp.zeros((1, LANES), F32).at[0, fa_lane0:fa_lane0 + h_fox].set(forget_bias[0].astype(F32))
    dt, dtm = _decay(slab, slabm, bias_row, fa_lane0, h_fox, n_meta)
    o_fox = _attention("fox", fox3, fox3m, h_fox, hd, 512, (dt, dtm, n_meta))

    zpad = jnp.zeros(slab.shape[:-1] + (LANES - D_IDX,), BF16)
    ki = slab[..., :D_IDX].astype(BF16)
    kie, kio = jnp.concatenate([ki, zpad], axis=-1), jnp.concatenate([zpad, ki], axis=-1)
    kim = slabm[:, :D_IDX].astype(BF16)
    kiem = jnp.concatenate([kim, zpad[0, :META_PAD]], axis=-1)
    kiom = jnp.concatenate([zpad[0, :META_PAD], kim], axis=-1)
    mask = _indexer_mask(qi, slab, kie, kio, kiem, kiom, wi_lane0, n_meta, k_top)
    band = _band_tables(rel_bias, n_meta)
    o_dsa = _attention("dsa", dsa3, dsa3m, h_dsa, hd, 512, (mask, band))

    mixed = _branch_merge(o_fox.reshape(b * L, w_fox), o_dsa.reshape(b * L, w_dsa),
                          w_bf_b, w_bd_b, gate, BF16)
    h2 = _mm(mixed, w_out_b, F32, res=x2)

    u2 = _rmsnorm(h2, mlp_norm_g[0], BF16)
    a, w_down_b = _mm(u2, w_up_b, BF16, act="relu2", casts=(w_down[0],))
    h3 = _mm_acc(a, w_down_b, h2, F32)
    return _rmsnorm(h3, final_norm_g, x.dtype).reshape(b, L, d)
```

```python
import functools
import math

import jax
import jax.numpy as jnp
from jax import lax
from jax.experimental import pallas as pl
from jax.experimental.pallas import tpu as pltpu

F32 = jnp.float32
BF16 = jnp.bfloat16

CHUNK = 64
H_IDX = 32
D_IDX = 64
TOPK_MAX = 256
MAX_DISTANCE = 128
RMS_EPS = 1e-6

LANES = 128
SUBLANES = 8
BF16_SUBLANES = 16
META_PAD = 128
NEG = -1e30
LOG2E = math.log2(math.e)
INT_MIN = -(2 ** 31)
KEY_NEG_INF = -2139095041
VMEM_LIMIT = 56 * 1024 * 1024
VMEM_LIMIT_BIG = 60 * 1024 * 1024


def _tile(dim, pref):
    if dim <= pref:
        return dim
    for t in range(pref - pref % LANES, 0, -LANES):
        if dim % t == 0:
            return t
    raise ValueError((dim, pref))


def _cparams(*sem, vmem=VMEM_LIMIT):
    return pltpu.CompilerParams(dimension_semantics=sem, vmem_limit_bytes=vmem)


def _rmsnorm_kernel(x_ref, g_ref, o_ref):
    x = x_ref[...].astype(F32)
    ms = jnp.mean(x * x, axis=-1, keepdims=True)
    y = x * lax.rsqrt(ms + RMS_EPS)
    o_ref[...] = (y * g_ref[...]).astype(o_ref.dtype)


def _rmsnorm(x, g, out_dtype):
    m, d = x.shape
    tr = _tile(m, 256)
    return pl.pallas_call(
        _rmsnorm_kernel,
        grid=(m // tr,),
        in_specs=[pl.BlockSpec((tr, d), lambda i: (i, 0)), pl.BlockSpec((1, d), lambda i: (0, 0))],
        out_specs=pl.BlockSpec((tr, d), lambda i: (i, 0)),
        out_shape=jax.ShapeDtypeStruct((m, d), out_dtype),
        compiler_params=_cparams("arbitrary"),
        name="rmsnorm",
    )(x, g.reshape(1, d).astype(F32))


def _columns_kernel(*refs):
    *w_refs, o_ref = refs
    parts = [r[...] for r in w_refs]
    pad = o_ref.shape[1] - sum(p.shape[0] for p in parts)
    if pad:
        parts.append(jnp.zeros((pad, parts[0].shape[1]), parts[0].dtype))
    rows = parts[0] if len(parts) == 1 else jnp.concatenate(parts, axis=0)
    o_ref[...] = rows.T.astype(o_ref.dtype)


def _columns_bf16(wt, ranges, chunk=512):
    n, k = wt.shape
    if len(ranges) == 1 and ranges[0][1] > chunk:
        (lo, width), = ranges
        chunk = _tile(width, chunk)
        steps, widths, out_w = width // chunk, [chunk], chunk
        offs = [lo]
    else:
        steps, widths = 1, [w for _, w in ranges]
        offs = [lo for lo, _ in ranges]
        out_w = -(-sum(widths) // LANES) * LANES
    assert all(o % SUBLANES == 0 and w % SUBLANES == 0 for o, w in zip(offs, widths))
    in_specs = [pl.BlockSpec((pl.Element(w), pl.Element(k)),
                             functools.partial(lambda i, o, w: (pl.multiple_of(o + i * w, SUBLANES), 0), o=o, w=w))
                for o, w in zip(offs, widths)]
    return pl.pallas_call(
        _columns_kernel,
        grid=(steps,),
        in_specs=in_specs,
        out_specs=pl.BlockSpec((k, out_w), lambda i: (0, i)),
        out_shape=jax.ShapeDtypeStruct((k, out_w * steps), BF16),
        compiler_params=_cparams("arbitrary"),
        name="weight_columns",
    )(*([wt] * len(offs)))


def _act(acc, act):
    if act == "sigmoid":
        return jax.nn.sigmoid(acc)
    if act == "relu2":
        r = jnp.maximum(acc, 0.0)
        return r * r
    return acc


def _mm_kernel(*refs, act, has_norm, emit_u, has_res, has_narrow, n_cast):
    refs = list(refs)
    a_ref = refs.pop(0)
    g_ref = refs.pop(0) if has_norm else None
    b_ref = refs.pop(0)
    r_ref = refs.pop(0) if has_res else None
    bn_ref = refs.pop(0) if has_narrow else None
    cast_in = [refs.pop(0) for _ in range(n_cast)]
    o_ref = refs.pop(0)
    u_out_ref = refs.pop(0) if emit_u else None
    on_ref = refs.pop(0) if has_narrow else None
    cast_out = [refs.pop(0) for _ in range(n_cast)]
    u_ref = refs.pop(0) if has_norm else a_ref

    if has_norm:
        @pl.when(pl.program_id(1) == 0)
        def _():
            rows = min(LANES, a_ref.shape[0])
            for r in range(0, a_ref.shape[0], rows):
                x = a_ref[r:r + rows, :].astype(F32)
                ms = jnp.mean(x * x, axis=-1, keepdims=True)
                u = (x * lax.rsqrt(ms + RMS_EPS) * g_ref[...]).astype(u_ref.dtype)
                u_ref[r:r + rows, :] = u
                if emit_u:
                    u_out_ref[r:r + rows, :] = u

    acc = _act(jnp.dot(u_ref[...], b_ref[...], preferred_element_type=F32), act)
    if has_res:
        acc = r_ref[...] + acc
    o_ref[...] = acc.astype(o_ref.dtype)
    if has_narrow:
        @pl.when(pl.program_id(1) == 0)
        def _():
            on_ref[...] = jnp.dot(u_ref[...], bn_ref[...], preferred_element_type=F32).astype(on_ref.dtype)
    for ci, co in zip(cast_in, cast_out):
        co[...] = ci[...].astype(co.dtype)


def _mm(a, b, out_dtype, act=None, res=None, tm=1024, tn=512, narrow=None, casts=(), norm_gain=None,
        emit_u=False):
    m, k = a.shape
    _, n = b.shape
    has_norm = norm_gain is not None
    assert has_norm or not emit_u
    tm, tn = _tile(m, tm), _tile(n, tn)
    ni, nj = m // tm, n // tn
    in_specs = [pl.BlockSpec((tm, k), lambda i, j: (i, 0))]
    args = [a]
    if has_norm:
        in_specs.append(pl.BlockSpec((1, k), lambda i, j: (0, 0)))
        args.append(norm_gain.reshape(1, k).astype(F32))
    in_specs.append(pl.BlockSpec((k, tn), lambda i, j: (0, j)))
    args.append(b)
    out_specs = [pl.BlockSpec((tm, tn), lambda i, j: (i, j))]
    out_shape = [jax.ShapeDtypeStruct((m, n), out_dtype)]
    if emit_u:
        out_specs.append(pl.BlockSpec((tm, k), lambda i, j: (i, 0)))
        out_shape.append(jax.ShapeDtypeStruct((m, k), BF16))
    if res is not None:
        in_specs.append(pl.BlockSpec((tm, tn), lambda i, j: (i, j)))
        args.append(res)
    if narrow is not None:
        in_specs.append(pl.BlockSpec((k, LANES), lambda i, j: (0, 0)))
        args.append(narrow)
        out_specs.append(pl.BlockSpec((tm, LANES), lambda i, j: (i, 0)))
        out_shape.append(jax.ShapeDtypeStruct((m, LANES), F32))
    riding = []
    for c in casts:
        rows, cols = c.shape
        slab = rows // (ni * nj)
        riding.append(slab * ni * nj == rows and slab % BF16_SUBLANES == 0)
        if riding[-1]:
            spec = pl.BlockSpec((slab, cols), lambda i, j: (i * nj + j, 0))
            in_specs.append(spec)
            args.append(c)
            out_specs.append(spec)
            out_shape.append(jax.ShapeDtypeStruct((rows, cols), BF16))
    outs = list(pl.pallas_call(
        functools.partial(_mm_kernel, act=act, has_norm=has_norm, emit_u=emit_u, has_res=res is not None,
                          has_narrow=narrow is not None, n_cast=sum(riding)),
        grid=(ni, nj),
        in_specs=in_specs,
        out_specs=out_specs,
        out_shape=out_shape,
        scratch_shapes=[pltpu.VMEM((tm, k), BF16)] if has_norm else [],
        compiler_params=_cparams("arbitrary", "arbitrary"),
        name="matmul",
    )(*args))
    n_main = len(outs) - sum(riding)
    cast_outs = iter(outs[n_main:])
    outs = outs[:n_main] + [next(cast_outs) if r else c.astype(BF16) for r, c in zip(riding, casts)]
    return outs[0] if len(outs) == 1 else tuple(outs)


def _mm_acc_kernel(a_ref, b_ref, r_ref, o_ref):
    kk = pl.program_id(2)
    part = jnp.dot(a_ref[...], b_ref[...], preferred_element_type=F32)

    @pl.when(kk == 0)
    def _():
        o_ref[...] = r_ref[...] + part

    @pl.when(kk > 0)
    def _():
        o_ref[...] += part


def _mm_acc(a, b, res, out_dtype, tm=1024, tn=1024, tk=4096):
    assert out_dtype == F32 and res.dtype == F32
    m, k = a.shape
    _, n = b.shape
    tm, tn, tk = _tile(m, tm), _tile(n, tn), _tile(k, tk)
    return pl.pallas_call(
        _mm_acc_kernel,
        grid=(m // tm, n // tn, k // tk),
        in_specs=[
            pl.BlockSpec((tm, tk), lambda i, j, kk: (i, kk)),
            pl.BlockSpec((tk, tn), lambda i, j, kk: (kk, j)),
            pl.BlockSpec((tm, tn), lambda i, j, kk: (i, j)),
        ],
        out_specs=pl.BlockSpec((tm, tn), lambda i, j, kk: (i, j)),
        out_shape=jax.ShapeDtypeStruct((m, n), out_dtype),
        compiler_params=_cparams("arbitrary", "arbitrary", "arbitrary", vmem=VMEM_LIMIT_BIG),
        name="matmul_acc",
    )(a, b, res)


def _branch_kernel(of_ref, od_ref, wf_ref, wd_ref, g0_ref, g1_ref, o_ref):
    yf = jnp.dot(of_ref[...], wf_ref[...], preferred_element_type=F32)
    yd = jnp.dot(od_ref[...], wd_ref[...], preferred_element_type=F32)
    o_ref[...] = (g0_ref[...] * yf + g1_ref[...] * yd).astype(o_ref.dtype)


def _branch_merge(o_fox, o_dsa, wf, wd, gate, out_dtype, tm=512, tn=1024):
    m, kf = o_fox.shape
    _, kd = o_dsa.shape
    n = wf.shape[1]
    tm, tn = _tile(m, tm), _tile(n, tn)
    nj = n // tn
    return pl.pallas_call(
        _branch_kernel,
        grid=(m // tm, nj),
        in_specs=[
            pl.BlockSpec((tm, kf), lambda i, j: (i, 0)),
            pl.BlockSpec((tm, kd), lambda i, j: (i, 0)),
            pl.BlockSpec((kf, tn), lambda i, j: (0, j)),
            pl.BlockSpec((kd, tn), lambda i, j: (0, j)),
            pl.BlockSpec((tm, tn), lambda i, j: (i, j)),
            pl.BlockSpec((tm, tn), lambda i, j: (i, j + nj)),
        ],
        out_specs=pl.BlockSpec((tm, tn), lambda i, j: (i, j)),
        out_shape=jax.ShapeDtypeStruct((m, n), out_dtype),
        compiler_params=_cparams("arbitrary", "arbitrary"),
        name="branch_merge",
    )(o_fox, o_dsa, wf, wd, gate, gate)


def _split3(x):
    hi = x.astype(BF16)
    r1 = x - hi.astype(F32)
    mid = r1.astype(BF16)
    lo = (r1 - mid.astype(F32)).astype(BF16)
    return hi, mid, lo


def _cumsum_lanes(xt, upper):
    out = None
    for part in _split3(xt):
        d = jnp.dot(part, upper, preferred_element_type=F32)
        out = d if out is None else out + d
    return out


def _decay_kernel(slab_ref, slabm_ref, bias_ref, dt_ref, dtm_ref, *, blk, lane0, nh, n_meta):
    L = slab_ref.shape[1]
    bias = bias_ref[...]

    def upper(n):
        r = lax.broadcasted_iota(jnp.int32, (n, n), 0)
        c = lax.broadcasted_iota(jnp.int32, (n, n), 1)
        return (r <= c).astype(BF16)

    xm = jax.nn.log_sigmoid(slabm_ref[...] + bias)
    dm = _cumsum_lanes(xm.T, upper(META_PAD))
    dtm_ref[0] = dm[lane0:lane0 + nh, :] * -LOG2E
    carry = dm[:, n_meta - 1:n_meta]
    up = upper(blk)
    for i in range(L // blk):
        x = jax.nn.log_sigmoid(slab_ref[0, i * blk:(i + 1) * blk, :] + bias)
        d = _cumsum_lanes(x.T, up) + carry
        dt_ref[0, :, i * blk:(i + 1) * blk] = d[lane0:lane0 + nh, :] * -LOG2E
        carry = d[:, blk - 1:blk]


def _decay(slab, slabm, bias_row, lane0, nh, n_meta):
    b, L, _ = slab.shape
    blk = _tile(L, 256)
    return pl.pallas_call(
        functools.partial(_decay_kernel, blk=blk, lane0=lane0, nh=nh, n_meta=n_meta),
        grid=(b,),
        in_specs=[
            pl.BlockSpec((1, L, LANES), lambda i: (i, 0, 0)),
            pl.BlockSpec((META_PAD, LANES), lambda i: (0, 0)),
            pl.BlockSpec((1, LANES), lambda i: (0, 0)),
        ],
        out_specs=[
            pl.BlockSpec((1, nh, L), lambda i: (i, 0, 0)),
            pl.BlockSpec((1, nh, META_PAD), lambda i: (i, 0, 0)),
        ],
        out_shape=[
            jax.ShapeDtypeStruct((b, nh, L), F32),
            jax.ShapeDtypeStruct((b, nh, META_PAD), F32),
        ],
        compiler_params=_cparams("arbitrary"),
        name="forget_cumsum",
    )(slab, slabm, bias_row)


def _dot_nt(a, b):
    return lax.dot_general(a, b, (((1,), (1,)), ((), ())), preferred_element_type=F32)


def _online_softmax_step(h, hd, s2, v, m_ref, l_ref, acc_ref):
    ts = s2.shape[1]
    m_prev = m_ref[h]
    m_new = jnp.maximum(m_prev, jnp.max(s2, axis=1, keepdims=True))
    alpha = jnp.exp2(m_prev - m_new)
    p = jnp.exp2(s2 - jnp.tile(m_new, (1, ts // LANES))).astype(BF16)
    v_ones = jnp.concatenate([v, jnp.ones((ts, LANES), v.dtype)], axis=1)
    pv = jnp.dot(p, v_ones, preferred_element_type=F32)
    hs = slice(h * hd, (h + 1) * hd)
    acc_ref[:, hs] = acc_ref[:, hs] * jnp.tile(alpha, (1, hd // LANES)) + pv[:, :hd]
    l_ref[h] = alpha * l_ref[h] + pv[:, hd:]
    m_ref[h] = m_new


def _attn_init(m_ref, l_ref, acc_ref):
    m_ref[...] = jnp.full_like(m_ref, NEG)
    l_ref[...] = jnp.zeros_like(l_ref)
    acc_ref[...] = jnp.zeros_like(acc_ref)


def _attn_finish(o_ref, l_ref, acc_ref, nh, hd):
    for h in range(nh):
        hs = slice(h * hd, (h + 1) * hd)
        o_ref[0, :, hs] = (acc_ref[:, hs] / jnp.tile(l_ref[h], (1, hd // LANES))).astype(o_ref.dtype)


def _fox_kernel(qt_ref, kt_ref, q_ref, k_ref, v_ref, km_ref, vm_ref, dt_ref, dtm_ref, o_ref, m_ref, l_ref, acc_ref,
                *, nh, hd, n_meta):
    step = pl.program_id(1)
    qt, kt = qt_ref[step], kt_ref[step]
    tq, ts = q_ref.shape[1], k_ref.shape[1]
    scale = hd ** -0.5 * LOG2E

    @pl.when(kt == 0)
    def _():
        _attn_init(m_ref, l_ref, acc_ref)

    def frames(diag):
        if diag:
            row = lax.broadcasted_iota(jnp.int32, (tq, ts), 0)
            col = lax.broadcasted_iota(jnp.int32, (tq, ts), 1)
            causal = col <= row
            colm = lax.broadcasted_iota(jnp.int32, (tq, META_PAD), 1)
            pad = jnp.where(colm < n_meta, 0.0, NEG).astype(F32)
        for h in range(nh):
            hs = slice(h * hd, (h + 1) * hd)
            q = q_ref[0, :, hs]
            s = _dot_nt(q, k_ref[0, :, hs]) * scale + dt_ref[0, h:h + 1, :]
            v = v_ref[0, :, hs]
            if diag:
                sm = _dot_nt(q, km_ref[:, hs]) * scale + dtm_ref[0, h:h + 1, :] + pad
                s = jnp.concatenate([jnp.where(causal, s, NEG), sm], axis=1)
                v = jnp.concatenate([v, vm_ref[:, hs]], axis=0)
            _online_softmax_step(h, hd, s, v, m_ref, l_ref, acc_ref)

    @pl.when(kt < qt)
    def _():
        frames(False)

    @pl.when(kt == qt)
    def _():
        frames(True)
        _attn_finish(o_ref, l_ref, acc_ref, nh, hd)


def _add_blocks(s, adds):
    rows = []
    for ib in range(s.shape[0] // LANES):
        r = s[ib * LANES:(ib + 1) * LANES]
        if any(i == ib for i, _ in adds):
            r = jnp.concatenate(
                [r[:, jb * LANES:(jb + 1) * LANES] + adds[(ib, jb)] if (ib, jb) in adds
                 else r[:, jb * LANES:(jb + 1) * LANES] for jb in range(s.shape[1] // LANES)], axis=1)
        rows.append(r)
    return jnp.concatenate(rows, axis=0)


def _dsa_kernel(qt_ref, kt_ref, q_ref, k_ref, v_ref, km_ref, vm_ref, mask_ref, maskm_ref, band_ref, o_ref,
                m_ref, l_ref, acc_ref, *, nh, hd):
    step = pl.program_id(1)
    qt, kt = qt_ref[step], kt_ref[step]
    tq, ts = q_ref.shape[1], k_ref.shape[1]
    nb = tq // LANES
    scale = hd ** -0.5 * LOG2E

    @pl.when(kt == 0)
    def _():
        _attn_init(m_ref, l_ref, acc_ref)

    def frames(kind):
        maskf = mask_ref[0].astype(F32)
        if kind == "diag":
            maskm = maskm_ref[0].astype(F32)
            first = (qt == 0).astype(F32)
        for h in range(nh):
            hs = slice(h * hd, (h + 1) * hd)
            q = q_ref[0, :, hs]
            s = _dot_nt(q, k_ref[0, :, hs]) * scale + maskf
            v = v_ref[0, :, hs]
            if kind == "near":
                s = _add_blocks(s, {(0, ts // LANES - 1): band_ref[1, h]})
            elif kind == "diag":
                sm = _dot_nt(q, km_ref[:, hs]) * scale + maskm
                s = jnp.concatenate([s, sm], axis=1)
                adds = {(0, ts // LANES): band_ref[2, h] * first}
                for ib in range(nb):
                    adds[(ib, ib)] = band_ref[0, h]
                    if ib > 0:
                        adds[(ib, ib - 1)] = band_ref[1, h]
                s = _add_blocks(s, adds)
                v = jnp.concatenate([v, vm_ref[:, hs]], axis=0)
            _online_softmax_step(h, hd, s, v, m_ref, l_ref, acc_ref)

    @pl.when(kt < qt - 1)
    def _():
        frames("far")

    @pl.when(kt == qt - 1)
    def _():
        frames("near")

    @pl.when(kt == qt)
    def _():
        frames("diag")
        _attn_finish(o_ref, l_ref, acc_ref, nh, hd)


def _attention(kind, big, bigm, nh, hd, tile, extra):
    b, L, _ = big.shape
    w = nh * hd
    q_blk, k_blk, v_blk = 0, 1, 2
    t = _tile(L, tile)
    nq = L // t
    pairs = [(qi, ki) for qi in range(nq) for ki in range(qi + 1)]
    qt_tab = jnp.asarray([p[0] for p in pairs], jnp.int32)
    kt_tab = jnp.asarray([p[1] for p in pairs], jnp.int32)
    q_spec = pl.BlockSpec((1, t, w), lambda bi, s, qt, kt: (bi, qt[s], q_blk))
    kv = lambda blk: pl.BlockSpec((1, t, w), lambda bi, s, qt, kt: (bi, kt[s], blk))
    kvm = lambda blk: pl.BlockSpec((META_PAD, w), lambda bi, s, qt, kt: (0, blk))
    scratch = [
        pltpu.VMEM((nh, t, LANES), F32),
        pltpu.VMEM((nh, t, LANES), F32),
        pltpu.VMEM((t, w), F32),
    ]
    if kind == "fox":
        dt, dtm, n_meta = extra
        body = functools.partial(_fox_kernel, nh=nh, hd=hd, n_meta=n_meta)
        in_specs = [q_spec, kv(k_blk), kv(v_blk), kvm(k_blk), kvm(v_blk),
                    pl.BlockSpec((1, nh, t), lambda bi, s, qt, kt: (bi, 0, kt[s])),
                    pl.BlockSpec((1, nh, META_PAD), lambda bi, s, qt, kt: (bi, 0, 0))]
        args = [big, big, big, bigm, bigm, dt, dtm]
    else:
        mask, band = extra
        body = functools.partial(_dsa_kernel, nh=nh, hd=hd)
        in_specs = [q_spec, kv(k_blk), kv(v_blk), kvm(k_blk), kvm(v_blk),
                    pl.BlockSpec((1, t, t), lambda bi, s, qt, kt: (bi, qt[s], kt[s])),
                    pl.BlockSpec((1, t, META_PAD), lambda bi, s, qt, kt: (bi, qt[s], L // META_PAD)),
                    pl.BlockSpec((3, nh, LANES, LANES), lambda bi, s, qt, kt: (0, 0, 0, 0))]
        args = [big, big, big, bigm, bigm, mask, mask, band]
    return pl.pallas_call(
        body,
        grid_spec=pltpu.PrefetchScalarGridSpec(
            num_scalar_prefetch=2,
            grid=(b, len(pairs)),
            in_specs=in_specs,
            out_specs=pl.BlockSpec((1, t, w), lambda bi, s, qt, kt: (bi, qt[s], 0)),
            scratch_shapes=scratch,
        ),
        out_shape=jax.ShapeDtypeStruct((b, L, w), BF16),
        compiler_params=_cparams("arbitrary", "arbitrary"),
        name=kind + "_attention",
    )(qt_tab, kt_tab, *args)


def _sortable(x):
    bits = pltpu.bitcast(x, jnp.int32)
    return bits ^ ((bits >> 31) & 0x7FFFFFFF)


def _index_kernel(qi_ref, w_ref, kie_ref, kio_ref, kiem_ref, kiom_ref, mask_ref, wb_ref, st_ref, smt_ref,
                  *, tc, w_lane0, n_meta, k_top):
    qt = pl.program_id(1)
    tq = qi_ref.shape[1]
    L = kie_ref.shape[1]
    n_ct = (qt + 1) * tq // tc
    npair = H_IDX // 2
    wscale = (H_IDX ** -0.5) * (D_IDX ** -0.5)

    wv = w_ref[0]
    for h in range(H_IDX):
        col = wv[:, w_lane0 + h:w_lane0 + h + 1] * wscale
        wb_ref[h] = jnp.broadcast_to(col, (tq, LANES))

    def score_tile(ke, ko):
        n = ke.shape[0]
        accs = [jnp.zeros((tq, LANES), F32) for _ in range(n // LANES)]
        for j in range(npair):
            lhs = qi_ref[0, :, j * LANES:(j + 1) * LANES]
            de = jnp.maximum(_dot_nt(lhs, ke), 0.0)
            do = jnp.maximum(_dot_nt(lhs, ko), 0.0)
            for c in range(n // LANES):
                cs = slice(c * LANES, (c + 1) * LANES)
                accs[c] = accs[c] + wb_ref[2 * j] * de[:, cs] + wb_ref[2 * j + 1] * do[:, cs]
        return accs[0] if len(accs) == 1 else jnp.concatenate(accs, axis=1)

    colm = lax.broadcasted_iota(jnp.int32, (tq, META_PAD), 1)
    sm = jnp.where(colm < n_meta, score_tile(kiem_ref[...], kiom_ref[...]), -jnp.inf)
    smt_ref[...] = _sortable(sm.T)

    qchunk = (qt * tq + lax.broadcasted_iota(jnp.int32, (tq, tc), 0)) // CHUNK

    def tile_body(c, carry):
        start = pl.multiple_of(c * tc, tc)
        sc = score_tile(kie_ref[0, pl.ds(start, tc), :], kio_ref[0, pl.ds(start, tc), :])
        kchunk = (start + lax.broadcasted_iota(jnp.int32, (tq, tc), 1)) // CHUNK
        sc = jnp.where(kchunk <= qchunk, sc, -jnp.inf)
        st_ref[c] = _sortable(sc.T)
        return carry

    lax.fori_loop(0, n_ct, tile_body, 0)

    sub = 8
    nacc = 4

    def count_tile(tile, thr8, accs):
        for r in range(tile.shape[0] // sub):
            accs[r % nacc] = accs[r % nacc] + (tile[r * sub:(r + 1) * sub, :] >= thr8).astype(jnp.int32)
        return accs

    def count_ge(thr8):
        accs = count_tile(smt_ref[...], thr8, [jnp.zeros((sub, tq), jnp.int32) for _ in range(nacc)])
        accs = lax.fori_loop(0, n_ct, lambda c, a: tuple(count_tile(st_ref[c], thr8, list(a))), tuple(accs))
        total = functools.reduce(lambda x, y: x + y, accs)
        return jnp.broadcast_to(jnp.sum(total, axis=0, keepdims=True), (sub, tq))

    def bit_body(it, ans_u):
        trial_u = ans_u | (jnp.int32(1) << (31 - it))
        cnt = count_ge(trial_u ^ INT_MIN)
        return jnp.where(cnt >= k_top, trial_u, ans_u)

    ans_u = lax.fori_loop(0, 32, bit_body, jnp.zeros((sub, tq), jnp.int32))
    thr8 = jnp.maximum(ans_u ^ INT_MIN, KEY_NEG_INF + 1)
    thr = thr8[:1]

    cnt_ge = count_ge(thr8)
    tied = jnp.max(((cnt_ge > k_top) & (thr8 > KEY_NEG_INF + 1)).astype(jnp.int32)) > 0

    @pl.when(tied)
    def _():
        need = k_top - count_ge(thr8 + 1)[:1]
        posm = lax.broadcasted_iota(jnp.int32, (META_PAD, tq), 0)
        posf = n_meta + lax.broadcasted_iota(jnp.int32, (tc, tq), 0)

        def tied_before(cut):
            acc = jnp.sum(((smt_ref[...] == thr) & (posm < cut)).astype(jnp.int32), axis=0, keepdims=True)

            def body(c, acc):
                hit = (st_ref[c] == thr) & (posf + c * tc < cut)
                return acc + jnp.sum(hit.astype(jnp.int32), axis=0, keepdims=True)

            return lax.fori_loop(0, n_ct, body, acc)

        nbits = (L + META_PAD).bit_length()

        def pos_body(it, cut):
            trial = cut | (jnp.int32(1) << (nbits - 1 - it))
            return jnp.where(tied_before(trial) < need, trial, cut)

        cut = lax.fori_loop(0, nbits, pos_body, jnp.zeros((1, tq), jnp.int32))
        smt_ref[...] = smt_ref[...] - ((smt_ref[...] == thr) & (posm > cut)).astype(jnp.int32)

        def lower(c, carry):
            st_ref[c] = st_ref[c] - ((st_ref[c] == thr) & (posf + c * tc > cut)).astype(jnp.int32)
            return carry

        lax.fori_loop(0, n_ct, lower, 0)

    def mask_of(keys_t):
        return jnp.where(keys_t >= thr, 0.0, NEG).astype(F32).T.astype(mask_ref.dtype)

    mask_ref[0, :, L:] = mask_of(smt_ref[...])
    for c in range(L // tc):
        cols = slice(c * tc, (c + 1) * tc)

        @pl.when(c < n_ct)
        def _():
            mask_ref[0, :, cols] = mask_of(st_ref[c])

        @pl.when(c >= n_ct)
        def _():
            mask_ref[0, :, cols] = jnp.full((tq, tc), NEG, mask_ref.dtype)


def _indexer_mask(qi, slab, kie, kio, kiem, kiom, w_lane0, n_meta, k_top, tq=256, tc=256):
    b, L, _ = qi.shape
    tq = _tile(L, tq)
    tc = _tile(tq, tc)
    wq = H_IDX * D_IDX
    return pl.pallas_call(
        functools.partial(_index_kernel, tc=tc, w_lane0=w_lane0, n_meta=n_meta, k_top=k_top),
        grid=(b, L // tq),
        in_specs=[
            pl.BlockSpec((1, tq, wq), lambda bi, qi: (bi, qi, 0)),
            pl.BlockSpec((1, tq, LANES), lambda bi, qi: (bi, qi, 0)),
            pl.BlockSpec((1, L, LANES), lambda bi, qi: (bi, 0, 0)),
            pl.BlockSpec((1, L, LANES), lambda bi, qi: (bi, 0, 0)),
            pl.BlockSpec((META_PAD, LANES), lambda bi, qi: (0, 0)),
            pl.BlockSpec((META_PAD, LANES), lambda bi, qi: (0, 0)),
        ],
        out_specs=pl.BlockSpec((1, tq, L + META_PAD), lambda bi, qi: (bi, qi, 0)),
        out_shape=jax.ShapeDtypeStruct((b, L, L + META_PAD), BF16),
        scratch_shapes=[
            pltpu.VMEM((H_IDX, tq, LANES), F32),
            pltpu.VMEM((L // tc, tc, tq), jnp.int32),
            pltpu.VMEM((META_PAD, tq), jnp.int32),
        ],
        compiler_params=_cparams("arbitrary", "arbitrary"),
        name="indexer_topk_mask",
    )(qi, slab, kie, kio, kiem, kiom)


def _t5_bucket(rel, n_buckets):
    half = n_buckets // 2
    max_exact = half // 2
    ret = jnp.where(rel > 0, half, 0)
    n = jnp.abs(rel)
    nf = jnp.maximum(n, 1).astype(jnp.float32)
    large = max_exact + (jnp.log(nf / max_exact) / math.log(MAX_DISTANCE / max_exact)
                         * (half - max_exact)).astype(jnp.int32)
    large = jnp.minimum(large, half - 1)
    return ret + jnp.where(n < max_exact, n, large)


def _band_kernel(bucket_ref, far_ref, rb_ref, o_ref, *, n_buckets, nh):
    bucket = bucket_ref[0]
    far = far_ref[0]
    for h in range(nh):
        acc = jnp.zeros(bucket.shape, F32)
        for bk in range(n_buckets):
            acc = jnp.where(bucket == bk, rb_ref[bk, h], acc)
        o_ref[0, h] = (acc - rb_ref[far, h]) * LOG2E


def _band_tables(rel_bias, n_meta):
    n_buckets, nh = rel_bias.shape
    i = jnp.arange(LANES, dtype=jnp.int32)[:, None]
    j = jnp.arange(LANES, dtype=jnp.int32)[None, :]
    rel = jnp.stack([j - i, j - i - LANES, j - n_meta - i])
    bucket = _t5_bucket(rel, n_buckets).astype(jnp.int32)
    far = _t5_bucket(jnp.full((1,), -MAX_DISTANCE, jnp.int32), n_buckets).astype(jnp.int32)
    return pl.pallas_call(
        functools.partial(_band_kernel, n_buckets=n_buckets, nh=nh),
        grid=(3,),
        in_specs=[
            pl.BlockSpec((1, LANES, LANES), lambda t: (t, 0, 0)),
            pl.BlockSpec(memory_space=pltpu.SMEM),
            pl.BlockSpec(memory_space=pltpu.SMEM),
        ],
        out_specs=pl.BlockSpec((1, nh, LANES, LANES), lambda t: (t, 0, 0, 0)),
        out_shape=jax.ShapeDtypeStruct((3, nh, LANES, LANES), F32),
        compiler_params=_cparams("arbitrary"),
        name="rel_bias_band",
    )(bucket, far, rel_bias.astype(F32))


def kernel(x, meta_tokens, attn_norm_g, w_in, forget_bias, rel_bias, w_branch_fox, w_branch_dsa,
           w_out, mlp_norm_g, w_up, w_down, final_norm_g):
    b, L, d = x.shape
    n_meta = meta_tokens.shape[0]
    h_fox = forget_bias.shape[1]
    h_dsa = rel_bias.shape[1]
    w_fox, w_dsa = w_branch_fox.shape[1], w_branch_dsa.shape[1]
    hd = w_fox // h_fox
    w_idx = H_IDX * D_IDX
    assert attn_norm_g.shape[0] == 1, "single-layer block"
    assert w_dsa // h_dsa == hd and hd % LANES == 0
    assert n_meta <= META_PAD and L % CHUNK == 0 and L % META_PAD == 0
    assert MAX_DISTANCE <= LANES and CHUNK <= LANES
    k_top = min(TOPK_MAX, L // 4)

    off_fa = 3 * w_fox
    off_dsa = off_fa + h_fox
    off_qi = off_dsa + 3 * w_dsa
    off_ki = off_qi + w_idx
    off_gate = off_ki + D_IDX + H_IDX
    assert off_gate + 2 * d == w_in.shape[2] and D_IDX + H_IDX + h_fox <= LANES
    wt = jnp.swapaxes(w_in, 1, 2)[0]
    w_fox3 = _columns_bf16(wt, [(0, 3 * w_fox)])
    w_dsa3 = _columns_bf16(wt, [(off_dsa, 3 * w_dsa)])
    w_qi = _columns_bf16(wt, [(off_qi, w_idx)])
    w_gate = _columns_bf16(wt, [(off_gate, 2 * d)])
    w_small = _columns_bf16(wt, [(off_ki, D_IDX + H_IDX), (off_fa, h_fox)])
    wi_lane0, fa_lane0 = D_IDX, D_IDX + H_IDX

    x2 = x.reshape(b * L, d)
    meta = jnp.zeros((META_PAD, d), x.dtype).at[:n_meta].set(meta_tokens.astype(x.dtype))
    um = _rmsnorm(meta, attn_norm_g[0], BF16)
    fox3, u = _mm(x2, w_fox3, BF16, tm=512, tn=1024, norm_gain=attn_norm_g[0], emit_u=True)
    fox3 = fox3.reshape(b, L, -1)
    fox3m = _mm(um, w_fox3, BF16)
    dsa3 = _mm(u, w_dsa3, BF16).reshape(b, L, -1)
    dsa3m = _mm(um, w_dsa3, BF16)
    qi, slab, w_out_b, w_bf_b, w_bd_b = _mm(u, w_qi, BF16, narrow=w_small,
                                            casts=(w_out[0], w_branch_fox[0], w_branch_dsa[0]))
    qi, slab = qi.reshape(b, L, -1), slab.reshape(b, L, LANES)
    gate, w_up_b = _mm(u, w_gate, F32, act="sigmoid", casts=(w_up[0],))
    slabm = _mm(um, w_small, F32, tn=LANES)

    bias_row = jnp.zeros((1, LANES), F32).at[0, fa_lane0:fa_lane0 + h_fox].set(forget_bias[0].astype(F32))
    dt, dtm = _decay(slab, slabm, bias_row, fa_lane0, h_fox, n_meta)
    o_fox = _attention("fox", fox3, fox3m, h_fox, hd, 512, (dt, dtm, n_meta))

    zpad = jnp.zeros(slab.shape[:-1] + (LANES - D_IDX,), BF16)
    ki = slab[..., :D_IDX].astype(BF16)
    kie, kio = jnp.concatenate([ki, zpad], axis=-1), jnp.concatenate([zpad, ki], axis=-1)
    kim = slabm[:, :D_IDX].astype(BF16)
    kiem = jnp.concatenate([kim, zpad[0, :META_PAD]], axis=-1)
    kiom = jnp.concatenate([zpad[0, :META_PAD], kim], axis=-1)
    mask = _indexer_mask(qi, slab, kie, kio, kiem, kiom, wi_lane0, n_meta, k_top)
    band = _band_tables(rel_bias, n_meta)
    o_dsa = _attention("dsa", dsa3, dsa3m, h_dsa, hd, 512, (mask, band))

    mixed = _branch_merge(o_fox.reshape(b * L, w_fox), o_dsa.reshape(b * L, w_dsa),
                          w_bf_b, w_bd_b, gate, BF16)
    h2 = _mm(mixed, w_out_b, F32, res=x2)

    a, w_down_b = _mm(h2, w_up_b, BF16, act="relu2", tm=512, tn=1024, norm_gain=mlp_norm_g[0],
                      casts=(w_down[0],))
    h3 = _mm_acc(a, w_down_b, h2, F32)
    return _rmsnorm(h3, final_norm_g, x.dtype).reshape(b, L, d)
```

```python
import functools
import math

import jax
import jax.numpy as jnp
from jax import lax
from jax.experimental import pallas as pl
from jax.experimental.pallas import tpu as pltpu

F32 = jnp.float32
BF16 = jnp.bfloat16

CHUNK = 64
H_IDX = 32
D_IDX = 64
TOPK_MAX = 256
MAX_DISTANCE = 128
RMS_EPS = 1e-6

LANES = 128
SUBLANES = 8
BF16_SUBLANES = 16
META_PAD = 128
NEG = -1e30
LOG2E = math.log2(math.e)
INT_MIN = -(2 ** 31)
KEY_NEG_INF = -2139095041
VMEM_LIMIT = 56 * 1024 * 1024
VMEM_LIMIT_BIG = 60 * 1024 * 1024


def _tile(dim, pref):
    if dim <= pref:
        return dim
    for t in range(pref - pref % LANES, 0, -LANES):
        if dim % t == 0:
            return t
    raise ValueError((dim, pref))


def _cparams(*sem, vmem=VMEM_LIMIT):
    return pltpu.CompilerParams(dimension_semantics=sem, vmem_limit_bytes=vmem)


def _rmsnorm_kernel(x_ref, g_ref, o_ref):
    x = x_ref[...].astype(F32)
    ms = jnp.mean(x * x, axis=-1, keepdims=True)
    y = x * lax.rsqrt(ms + RMS_EPS)
    o_ref[...] = (y * g_ref[...]).astype(o_ref.dtype)


def _rmsnorm(x, g, out_dtype):
    m, d = x.shape
    tr = _tile(m, 256)
    return pl.pallas_call(
        _rmsnorm_kernel,
        grid=(m // tr,),
        in_specs=[pl.BlockSpec((tr, d), lambda i: (i, 0)), pl.BlockSpec((1, d), lambda i: (0, 0))],
        out_specs=pl.BlockSpec((tr, d), lambda i: (i, 0)),
        out_shape=jax.ShapeDtypeStruct((m, d), out_dtype),
        compiler_params=_cparams("arbitrary"),
        name="rmsnorm",
    )(x, g.reshape(1, d).astype(F32))


def _columns_kernel(*refs):
    *w_refs, o_ref = refs
    parts = [r[...] for r in w_refs]
    pad = o_ref.shape[1] - sum(p.shape[0] for p in parts)
    if pad:
        parts.append(jnp.zeros((pad, parts[0].shape[1]), parts[0].dtype))
    rows = parts[0] if len(parts) == 1 else jnp.concatenate(parts, axis=0)
    o_ref[...] = rows.T.astype(o_ref.dtype)


def _columns_bf16(wt, ranges, chunk=512):
    n, k = wt.shape
    if len(ranges) == 1 and ranges[0][1] > chunk:
        (lo, width), = ranges
        chunk = _tile(width, chunk)
        steps, widths, out_w = width // chunk, [chunk], chunk
        offs = [lo]
    else:
        steps, widths = 1, [w for _, w in ranges]
        offs = [lo for lo, _ in ranges]
        out_w = -(-sum(widths) // LANES) * LANES
    assert all(o % SUBLANES == 0 and w % SUBLANES == 0 for o, w in zip(offs, widths))
    in_specs = [pl.BlockSpec((pl.Element(w), pl.Element(k)),
                             functools.partial(lambda i, o, w: (pl.multiple_of(o + i * w, SUBLANES), 0), o=o, w=w))
                for o, w in zip(offs, widths)]
    return pl.pallas_call(
        _columns_kernel,
        grid=(steps,),
        in_specs=in_specs,
        out_specs=pl.BlockSpec((k, out_w), lambda i: (0, i)),
        out_shape=jax.ShapeDtypeStruct((k, out_w * steps), BF16),
        compiler_params=_cparams("arbitrary"),
        name="weight_columns",
    )(*([wt] * len(offs)))


def _act(acc, act):
    if act == "sigmoid":
        return 0.5 * jnp.tanh(0.5 * acc) + 0.5
    if act == "relu2":
        r = jnp.maximum(acc, 0.0)
        return r * r
    return acc


def _mm_kernel(*refs, act, has_res, has_narrow, n_cast):
    refs = list(refs)
    a_ref = refs.pop(0)
    b_ref = refs.pop(0)
    r_ref = refs.pop(0) if has_res else None
    bn_ref = refs.pop(0) if has_narrow else None
    cast_in = [refs.pop(0) for _ in range(n_cast)]
    o_ref = refs.pop(0)
    on_ref = refs.pop(0) if has_narrow else None
    cast_out = [refs.pop(0) for _ in range(n_cast)]

    acc = _act(jnp.dot(a_ref[...], b_ref[...], preferred_element_type=F32), act)
    if has_res:
        acc = r_ref[...] + acc
    o_ref[...] = acc.astype(o_ref.dtype)
    if has_narrow:
        @pl.when(pl.program_id(1) == 0)
        def _():
            on_ref[...] = jnp.dot(a_ref[...], bn_ref[...], preferred_element_type=F32).astype(on_ref.dtype)
    for ci, co in zip(cast_in, cast_out):
        co[...] = ci[...].astype(co.dtype)


def _mm(a, b, out_dtype, act=None, res=None, tm=1024, tn=512, narrow=None, casts=()):
    m, k = a.shape
    _, n = b.shape
    tm, tn = _tile(m, tm), _tile(n, tn)
    ni, nj = m // tm, n // tn
    in_specs = [pl.BlockSpec((tm, k), lambda i, j: (i, 0)), pl.BlockSpec((k, tn), lambda i, j: (0, j))]
    args = [a, b]
    out_specs = [pl.BlockSpec((tm, tn), lambda i, j: (i, j))]
    out_shape = [jax.ShapeDtypeStruct((m, n), out_dtype)]
    if res is not None:
        in_specs.append(pl.BlockSpec((tm, tn), lambda i, j: (i, j)))
        args.append(res)
    if narrow is not None:
        in_specs.append(pl.BlockSpec((k, LANES), lambda i, j: (0, 0)))
        args.append(narrow)
        out_specs.append(pl.BlockSpec((tm, LANES), lambda i, j: (i, 0)))
        out_shape.append(jax.ShapeDtypeStruct((m, LANES), F32))
    riding = []
    for c in casts:
        rows, cols = c.shape
        slab = rows // (ni * nj)
        riding.append(slab * ni * nj == rows and slab % BF16_SUBLANES == 0)
        if riding[-1]:
            spec = pl.BlockSpec((slab, cols), lambda i, j: (i * nj + j, 0))
            in_specs.append(spec)
            args.append(c)
            out_specs.append(spec)
            out_shape.append(jax.ShapeDtypeStruct((rows, cols), BF16))
    outs = list(pl.pallas_call(
        functools.partial(_mm_kernel, act=act, has_res=res is not None, has_narrow=narrow is not None,
                          n_cast=sum(riding)),
        grid=(ni, nj),
        in_specs=in_specs,
        out_specs=out_specs,
        out_shape=out_shape,
        compiler_params=_cparams("arbitrary", "arbitrary"),
        name="matmul",
    )(*args))
    n_main = len(outs) - sum(riding)
    cast_outs = iter(outs[n_main:])
    outs = outs[:n_main] + [next(cast_outs) if r else c.astype(BF16) for r, c in zip(riding, casts)]
    return outs[0] if len(outs) == 1 else tuple(outs)


def _mm_acc_kernel(a_ref, b_ref, r_ref, o_ref):
    kk = pl.program_id(2)
    part = jnp.dot(a_ref[...], b_ref[...], preferred_element_type=F32)

    @pl.when(kk == 0)
    def _():
        o_ref[...] = r_ref[...] + part

    @pl.when(kk > 0)
    def _():
        o_ref[...] += part


def _mm_acc(a, b, res, out_dtype, tm=1024, tn=1024, tk=4096):
    assert out_dtype == F32 and res.dtype == F32
    m, k = a.shape
    _, n = b.shape
    tm, tn, tk = _tile(m, tm), _tile(n, tn), _tile(k, tk)
    return pl.pallas_call(
        _mm_acc_kernel,
        grid=(m // tm, n // tn, k // tk),
        in_specs=[
            pl.BlockSpec((tm, tk), lambda i, j, kk: (i, kk)),
            pl.BlockSpec((tk, tn), lambda i, j, kk: (kk, j)),
            pl.BlockSpec((tm, tn), lambda i, j, kk: (i, j)),
        ],
        out_specs=pl.BlockSpec((tm, tn), lambda i, j, kk: (i, j)),
        out_shape=jax.ShapeDtypeStruct((m, n), out_dtype),
        compiler_params=_cparams("arbitrary", "arbitrary", "arbitrary", vmem=VMEM_LIMIT_BIG),
        name="matmul_acc",
    )(a, b, res)


def _branch_kernel(of_ref, od_ref, wf_ref, wd_ref, g0_ref, g1_ref, o_ref):
    yf = jnp.dot(of_ref[...], wf_ref[...], preferred_element_type=F32)
    yd = jnp.dot(od_ref[...], wd_ref[...], preferred_element_type=F32)
    o_ref[...] = (g0_ref[...] * yf + g1_ref[...] * yd).astype(o_ref.dtype)


def _branch_merge(o_fox, o_dsa, wf, wd, gate, out_dtype, tm=1024, tn=512):
    m, kf = o_fox.shape
    _, kd = o_dsa.shape
    n = wf.shape[1]
    tm, tn = _tile(m, tm), _tile(n, tn)
    nj = n // tn
    return pl.pallas_call(
        _branch_kernel,
        grid=(m // tm, nj),
        in_specs=[
            pl.BlockSpec((tm, kf), lambda i, j: (i, 0)),
            pl.BlockSpec((tm, kd), lambda i, j: (i, 0)),
            pl.BlockSpec((kf, tn), lambda i, j: (0, j)),
            pl.BlockSpec((kd, tn), lambda i, j: (0, j)),
            pl.BlockSpec((tm, tn), lambda i, j: (i, j)),
            pl.BlockSpec((tm, tn), lambda i, j: (i, j + nj)),
        ],
        out_specs=pl.BlockSpec((tm, tn), lambda i, j: (i, j)),
        out_shape=jax.ShapeDtypeStruct((m, n), out_dtype),
        compiler_params=_cparams("arbitrary", "arbitrary"),
        name="branch_merge",
    )(o_fox, o_dsa, wf, wd, gate, gate)


def _split3(x):
    hi = x.astype(BF16)
    r1 = x - hi.astype(F32)
    mid = r1.astype(BF16)
    lo = (r1 - mid.astype(F32)).astype(BF16)
    return hi, mid, lo


def _cumsum_lanes(xt, upper):
    out = None
    for part in _split3(xt):
        d = jnp.dot(part, upper, preferred_element_type=F32)
        out = d if out is None else out + d
    return out


def _decay_kernel(slab_ref, slabm_ref, bias_ref, dt_ref, dtm_ref, *, blk, lane0, nh, n_meta):
    L = slab_ref.shape[1]
    bias = bias_ref[...]

    def upper(n):
        r = lax.broadcasted_iota(jnp.int32, (n, n), 0)
        c = lax.broadcasted_iota(jnp.int32, (n, n), 1)
        return (r <= c).astype(BF16)

    xm = jax.nn.log_sigmoid(slabm_ref[...] + bias)
    dm = _cumsum_lanes(xm.T, upper(META_PAD))
    dtm_ref[0] = dm[lane0:lane0 + nh, :] * -LOG2E
    carry = dm[:, n_meta - 1:n_meta]
    up = upper(blk)
    for i in range(L // blk):
        x = jax.nn.log_sigmoid(slab_ref[0, i * blk:(i + 1) * blk, :] + bias)
        d = _cumsum_lanes(x.T, up) + carry
        dt_ref[0, :, i * blk:(i + 1) * blk] = d[lane0:lane0 + nh, :] * -LOG2E
        carry = d[:, blk - 1:blk]


def _decay(slab, slabm, bias_row, lane0, nh, n_meta):
    b, L, _ = slab.shape
    blk = _tile(L, 256)
    return pl.pallas_call(
        functools.partial(_decay_kernel, blk=blk, lane0=lane0, nh=nh, n_meta=n_meta),
        grid=(b,),
        in_specs=[
            pl.BlockSpec((1, L, LANES), lambda i: (i, 0, 0)),
            pl.BlockSpec((META_PAD, LANES), lambda i: (0, 0)),
            pl.BlockSpec((1, LANES), lambda i: (0, 0)),
        ],
        out_specs=[
            pl.BlockSpec((1, nh, L), lambda i: (i, 0, 0)),
            pl.BlockSpec((1, nh, META_PAD), lambda i: (i, 0, 0)),
        ],
        out_shape=[
            jax.ShapeDtypeStruct((b, nh, L), F32),
            jax.ShapeDtypeStruct((b, nh, META_PAD), F32),
        ],
        compiler_params=_cparams("arbitrary"),
        name="forget_cumsum",
    )(slab, slabm, bias_row)


def _dot_nt(a, b):
    return lax.dot_general(a, b, (((1,), (1,)), ((), ())), preferred_element_type=F32)


def _online_softmax_step(h, hd, s2, v, m_ref, l_ref, acc_ref):
    ts = s2.shape[1]
    m_prev = m_ref[h]
    m_new = jnp.maximum(m_prev, jnp.max(s2, axis=1, keepdims=True))
    alpha = jnp.exp2(m_prev - m_new)
    p = jnp.exp2(s2 - jnp.tile(m_new, (1, ts // LANES))).astype(BF16)
    v_ones = jnp.concatenate([v, jnp.ones((ts, LANES), v.dtype)], axis=1)
    pv = jnp.dot(p, v_ones, preferred_element_type=F32)
    hs = slice(h * hd, (h + 1) * hd)
    acc_ref[:, hs] = acc_ref[:, hs] * jnp.tile(alpha, (1, hd // LANES)) + pv[:, :hd]
    l_ref[h] = alpha * l_ref[h] + pv[:, hd:]
    m_ref[h] = m_new


def _attn_init(m_ref, l_ref, acc_ref):
    m_ref[...] = jnp.full_like(m_ref, NEG)
    l_ref[...] = jnp.zeros_like(l_ref)
    acc_ref[...] = jnp.zeros_like(acc_ref)


def _attn_finish(o_ref, l_ref, acc_ref, nh, hd):
    for h in range(nh):
        hs = slice(h * hd, (h + 1) * hd)
        o_ref[0, :, hs] = (acc_ref[:, hs] / jnp.tile(l_ref[h], (1, hd // LANES))).astype(o_ref.dtype)


def _fox_kernel(qt_ref, kt_ref, q_ref, k_ref, v_ref, km_ref, vm_ref, dt_ref, dtm_ref, o_ref, m_ref, l_ref, acc_ref,
                *, nh, hd, n_meta):
    step = pl.program_id(1)
    qt, kt = qt_ref[step], kt_ref[step]
    tq, ts = q_ref.shape[1], k_ref.shape[1]
    scale = hd ** -0.5 * LOG2E

    @pl.when(kt == 0)
    def _():
        _attn_init(m_ref, l_ref, acc_ref)

    def frames(diag):
        if diag:
            row = lax.broadcasted_iota(jnp.int32, (tq, ts), 0)
            col = lax.broadcasted_iota(jnp.int32, (tq, ts), 1)
            causal = col <= row
            colm = lax.broadcasted_iota(jnp.int32, (tq, META_PAD), 1)
            pad = jnp.where(colm < n_meta, 0.0, NEG).astype(F32)
        for h in range(nh):
            hs = slice(h * hd, (h + 1) * hd)
            q = q_ref[0, :, hs]
            s = _dot_nt(q, k_ref[0, :, hs]) * scale + dt_ref[0, h:h + 1, :]
            v = v_ref[0, :, hs]
            if diag:
                sm = _dot_nt(q, km_ref[:, hs]) * scale + dtm_ref[0, h:h + 1, :] + pad
                s = jnp.concatenate([jnp.where(causal, s, NEG), sm], axis=1)
                v = jnp.concatenate([v, vm_ref[:, hs]], axis=0)
            _online_softmax_step(h, hd, s, v, m_ref, l_ref, acc_ref)

    @pl.when(kt < qt)
    def _():
        frames(False)

    @pl.when(kt == qt)
    def _():
        frames(True)
        _attn_finish(o_ref, l_ref, acc_ref, nh, hd)


def _add_blocks(s, adds):
    rows = []
    for ib in range(s.shape[0] // LANES):
        r = s[ib * LANES:(ib + 1) * LANES]
        if any(i == ib for i, _ in adds):
            r = jnp.concatenate(
                [r[:, jb * LANES:(jb + 1) * LANES] + adds[(ib, jb)] if (ib, jb) in adds
                 else r[:, jb * LANES:(jb + 1) * LANES] for jb in range(s.shape[1] // LANES)], axis=1)
        rows.append(r)
    return jnp.concatenate(rows, axis=0)


def _dsa_kernel(qt_ref, kt_ref, q_ref, k_ref, v_ref, km_ref, vm_ref, mask_ref, maskm_ref, band_ref, o_ref,
                m_ref, l_ref, acc_ref, *, nh, hd):
    step = pl.program_id(1)
    qt, kt = qt_ref[step], kt_ref[step]
    tq, ts = q_ref.shape[1], k_ref.shape[1]
    nb = tq // LANES
    scale = hd ** -0.5 * LOG2E

    @pl.when(kt == 0)
    def _():
        _attn_init(m_ref, l_ref, acc_ref)

    def frames(kind):
        maskf = mask_ref[0].astype(F32)
        if kind == "diag":
            maskm = maskm_ref[0].astype(F32)
            first = (qt == 0).astype(F32)
        for h in range(nh):
            hs = slice(h * hd, (h + 1) * hd)
            q = q_ref[0, :, hs]
            s = _dot_nt(q, k_ref[0, :, hs]) * scale + maskf
            v = v_ref[0, :, hs]
            if kind == "near":
                s = _add_blocks(s, {(0, ts // LANES - 1): band_ref[1, h]})
            elif kind == "diag":
                sm = _dot_nt(q, km_ref[:, hs]) * scale + maskm
                s = jnp.concatenate([s, sm], axis=1)
                adds = {(0, ts // LANES): band_ref[2, h] * first}
                for ib in range(nb):
                    adds[(ib, ib)] = band_ref[0, h]
                    if ib > 0:
                        adds[(ib, ib - 1)] = band_ref[1, h]
                s = _add_blocks(s, adds)
                v = jnp.concatenate([v, vm_ref[:, hs]], axis=0)
            _online_softmax_step(h, hd, s, v, m_ref, l_ref, acc_ref)

    @pl.when(kt < qt - 1)
    def _():
        frames("far")

    @pl.when(kt == qt - 1)
    def _():
        frames("near")

    @pl.when(kt == qt)
    def _():
        frames("diag")
        _attn_finish(o_ref, l_ref, acc_ref, nh, hd)


def _attention(kind, big, bigm, nh, hd, tile, extra):
    b, L, _ = big.shape
    w = nh * hd
    q_blk, k_blk, v_blk = 0, 1, 2
    t = _tile(L, tile)
    nq = L // t
    pairs = [(qi, ki) for qi in range(nq) for ki in range(qi + 1)]
    qt_tab = jnp.asarray([p[0] for p in pairs], jnp.int32)
    kt_tab = jnp.asarray([p[1] for p in pairs], jnp.int32)
    q_spec = pl.BlockSpec((1, t, w), lambda bi, s, qt, kt: (bi, qt[s], q_blk))
    kv = lambda blk: pl.BlockSpec((1, t, w), lambda bi, s, qt, kt: (bi, kt[s], blk))
    kvm = lambda blk: pl.BlockSpec((META_PAD, w), lambda bi, s, qt, kt: (0, blk))
    scratch = [
        pltpu.VMEM((nh, t, LANES), F32),
        pltpu.VMEM((nh, t, LANES), F32),
        pltpu.VMEM((t, w), F32),
    ]
    if kind == "fox":
        dt, dtm, n_meta = extra
        body = functools.partial(_fox_kernel, nh=nh, hd=hd, n_meta=n_meta)
        in_specs = [q_spec, kv(k_blk), kv(v_blk), kvm(k_blk), kvm(v_blk),
                    pl.BlockSpec((1, nh, t), lambda bi, s, qt, kt: (bi, 0, kt[s])),
                    pl.BlockSpec((1, nh, META_PAD), lambda bi, s, qt, kt: (bi, 0, 0))]
        args = [big, big, big, bigm, bigm, dt, dtm]
    else:
        mask, band = extra
        body = functools.partial(_dsa_kernel, nh=nh, hd=hd)
        in_specs = [q_spec, kv(k_blk), kv(v_blk), kvm(k_blk), kvm(v_blk),
                    pl.BlockSpec((1, t, t), lambda bi, s, qt, kt: (bi, qt[s], kt[s])),
                    pl.BlockSpec((1, t, META_PAD), lambda bi, s, qt, kt: (bi, qt[s], L // META_PAD)),
                    pl.BlockSpec((3, nh, LANES, LANES), lambda bi, s, qt, kt: (0, 0, 0, 0))]
        args = [big, big, big, bigm, bigm, mask, mask, band]
    return pl.pallas_call(
        body,
        grid_spec=pltpu.PrefetchScalarGridSpec(
            num_scalar_prefetch=2,
            grid=(b, len(pairs)),
            in_specs=in_specs,
            out_specs=pl.BlockSpec((1, t, w), lambda bi, s, qt, kt: (bi, qt[s], 0)),
            scratch_shapes=scratch,
        ),
        out_shape=jax.ShapeDtypeStruct((b, L, w), BF16),
        compiler_params=_cparams("arbitrary", "arbitrary"),
        name=kind + "_attention",
    )(qt_tab, kt_tab, *args)


def _sortable(x):
    bits = pltpu.bitcast(x, jnp.int32)
    return bits ^ ((bits >> 31) & 0x7FFFFFFF)


def _index_kernel(qi_ref, w_ref, kie_ref, kio_ref, kiem_ref, kiom_ref, mask_ref, wb_ref, st_ref, smt_ref,
                  *, tc, w_lane0, n_meta, k_top):
    qt = pl.program_id(1)
    tq = qi_ref.shape[1]
    L = kie_ref.shape[1]
    n_ct = (qt + 1) * tq // tc
    npair = H_IDX // 2
    wscale = (H_IDX ** -0.5) * (D_IDX ** -0.5)

    wv = w_ref[0]
    for h in range(H_IDX):
        col = wv[:, w_lane0 + h:w_lane0 + h + 1] * wscale
        wb_ref[h] = jnp.broadcast_to(col, (tq, LANES))

    def score_tile(ke, ko):
        n = ke.shape[0]
        accs = [jnp.zeros((tq, LANES), F32) for _ in range(n // LANES)]
        for j in range(npair):
            lhs = qi_ref[0, :, j * LANES:(j + 1) * LANES]
            de = jnp.maximum(_dot_nt(lhs, ke), 0.0)
            do = jnp.maximum(_dot_nt(lhs, ko), 0.0)
            for c in range(n // LANES):
                cs = slice(c * LANES, (c + 1) * LANES)
                accs[c] = accs[c] + wb_ref[2 * j] * de[:, cs] + wb_ref[2 * j + 1] * do[:, cs]
        return accs[0] if len(accs) == 1 else jnp.concatenate(accs, axis=1)

    colm = lax.broadcasted_iota(jnp.int32, (tq, META_PAD), 1)
    sm = jnp.where(colm < n_meta, score_tile(kiem_ref[...], kiom_ref[...]), -jnp.inf)
    smt_ref[...] = _sortable(sm.T)

    qchunk = (qt * tq + lax.broadcasted_iota(jnp.int32, (tq, tc), 0)) // CHUNK

    def tile_body(c, carry):
        start = pl.multiple_of(c * tc, tc)
        sc = score_tile(kie_ref[0, pl.ds(start, tc), :], kio_ref[0, pl.ds(start, tc), :])
        kchunk = (start + lax.broadcasted_iota(jnp.int32, (tq, tc), 1)) // CHUNK
        sc = jnp.where(kchunk <= qchunk, sc, -jnp.inf)
        st_ref[c] = _sortable(sc.T)
        return carry

    lax.fori_loop(0, n_ct, tile_body, 0)

    sub = 8
    nacc = 4

    def count_tile(tile, thr8, accs):
        for r in range(tile.shape[0] // sub):
            accs[r % nacc] = accs[r % nacc] + (tile[r * sub:(r + 1) * sub, :] >= thr8).astype(jnp.int32)
        return accs

    def count_ge(thr8):
        accs = count_tile(smt_ref[...], thr8, [jnp.zeros((sub, tq), jnp.int32) for _ in range(nacc)])
        accs = lax.fori_loop(0, n_ct, lambda c, a: tuple(count_tile(st_ref[c], thr8, list(a))), tuple(accs))
        total = functools.reduce(lambda x, y: x + y, accs)
        return jnp.broadcast_to(jnp.sum(total, axis=0, keepdims=True), (sub, tq))

    def bit_body(it, ans_u):
        trial_u = ans_u | (jnp.int32(1) << (31 - it))
        cnt = count_ge(trial_u ^ INT_MIN)
        return jnp.where(cnt >= k_top, trial_u, ans_u)

    ans_u = lax.fori_loop(0, 32, bit_body, jnp.zeros((sub, tq), jnp.int32))
    thr8 = jnp.maximum(ans_u ^ INT_MIN, KEY_NEG_INF + 1)
    thr = thr8[:1]

    cnt_ge = count_ge(thr8)
    tied = jnp.max(((cnt_ge > k_top) & (thr8 > KEY_NEG_INF + 1)).astype(jnp.int32)) > 0

    @pl.when(tied)
    def _():
        need = k_top - count_ge(thr8 + 1)[:1]
        posm = lax.broadcasted_iota(jnp.int32, (META_PAD, tq), 0)
        posf = n_meta + lax.broadcasted_iota(jnp.int32, (tc, tq), 0)

        def tied_before(cut):
            acc = jnp.sum(((smt_ref[...] == thr) & (posm < cut)).astype(jnp.int32), axis=0, keepdims=True)

            def body(c, acc):
                hit = (st_ref[c] == thr) & (posf + c * tc < cut)
                return acc + jnp.sum(hit.astype(jnp.int32), axis=0, keepdims=True)

            return lax.fori_loop(0, n_ct, body, acc)

        nbits = (L + META_PAD).bit_length()

        def pos_body(it, cut):
            trial = cut | (jnp.int32(1) << (nbits - 1 - it))
            return jnp.where(tied_before(trial) < need, trial, cut)

        cut = lax.fori_loop(0, nbits, pos_body, jnp.zeros((1, tq), jnp.int32))
        smt_ref[...] = smt_ref[...] - ((smt_ref[...] == thr) & (posm > cut)).astype(jnp.int32)

        def lower(c, carry):
            st_ref[c] = st_ref[c] - ((st_ref[c] == thr) & (posf + c * tc > cut)).astype(jnp.int32)
            return carry

        lax.fori_loop(0, n_ct, lower, 0)

    def mask_of(keys_t):
        return jnp.where(keys_t >= thr, 0.0, NEG).astype(F32).T.astype(mask_ref.dtype)

    mask_ref[0, :, L:] = mask_of(smt_ref[...])
    for c in range(L // tc):
        cols = slice(c * tc, (c + 1) * tc)

        @pl.when(c < n_ct)
        def _():
            mask_ref[0, :, cols] = mask_of(st_ref[c])

        @pl.when(c >= n_ct)
        def _():
            mask_ref[0, :, cols] = jnp.full((tq, tc), NEG, mask_ref.dtype)


def _indexer_mask(qi, slab, kie, kio, kiem, kiom, w_lane0, n_meta, k_top, tq=256, tc=256):
    b, L, _ = qi.shape
    tq = _tile(L, tq)
    tc = _tile(tq, tc)
    wq = H_IDX * D_IDX
    return pl.pallas_call(
        functools.partial(_index_kernel, tc=tc, w_lane0=w_lane0, n_meta=n_meta, k_top=k_top),
        grid=(b, L // tq),
        in_specs=[
            pl.BlockSpec((1, tq, wq), lambda bi, qi: (bi, qi, 0)),
            pl.BlockSpec((1, tq, LANES), lambda bi, qi: (bi, qi, 0)),
            pl.BlockSpec((1, L, LANES), lambda bi, qi: (bi, 0, 0)),
            pl.BlockSpec((1, L, LANES), lambda bi, qi: (bi, 0, 0)),
            pl.BlockSpec((META_PAD, LANES), lambda bi, qi: (0, 0)),
            pl.BlockSpec((META_PAD, LANES), lambda bi, qi: (0, 0)),
        ],
        out_specs=pl.BlockSpec((1, tq, L + META_PAD), lambda bi, qi: (bi, qi, 0)),
        out_shape=jax.ShapeDtypeStruct((b, L, L + META_PAD), BF16),
        scratch_shapes=[
            pltpu.VMEM((H_IDX, tq, LANES), F32),
            pltpu.VMEM((L // tc, tc, tq), jnp.int32),
            pltpu.VMEM((META_PAD, tq), jnp.int32),
        ],
        compiler_params=_cparams("arbitrary", "arbitrary"),
        name="indexer_topk_mask",
    )(qi, slab, kie, kio, kiem, kiom)


def _t5_bucket(rel, n_buckets):
    half = n_buckets // 2
    max_exact = half // 2
    ret = jnp.where(rel > 0, half, 0)
    n = jnp.abs(rel)
    nf = jnp.maximum(n, 1).astype(jnp.float32)
    large = max_exact + (jnp.log(nf / max_exact) / math.log(MAX_DISTANCE / max_exact)
                         * (half - max_exact)).astype(jnp.int32)
    large = jnp.minimum(large, half - 1)
    return ret + jnp.where(n < max_exact, n, large)


def _band_kernel(bucket_ref, far_ref, rb_ref, o_ref, *, n_buckets, nh):
    bucket = bucket_ref[0]
    far = far_ref[0]
    for h in range(nh):
        acc = jnp.zeros(bucket.shape, F32)
        for bk in range(n_buckets):
            acc = jnp.where(bucket == bk, rb_ref[bk, h], acc)
        o_ref[0, h] = (acc - rb_ref[far, h]) * LOG2E


def _band_tables(rel_bias, n_meta):
    n_buckets, nh = rel_bias.shape
    i = jnp.arange(LANES, dtype=jnp.int32)[:, None]
    j = jnp.arange(LANES, dtype=jnp.int32)[None, :]
    rel = jnp.stack([j - i, j - i - LANES, j - n_meta - i])
    bucket = _t5_bucket(rel, n_buckets).astype(jnp.int32)
    far = _t5_bucket(jnp.full((1,), -MAX_DISTANCE, jnp.int32), n_buckets).astype(jnp.int32)
    return pl.pallas_call(
        functools.partial(_band_kernel, n_buckets=n_buckets, nh=nh),
        grid=(3,),
        in_specs=[
            pl.BlockSpec((1, LANES, LANES), lambda t: (t, 0, 0)),
            pl.BlockSpec(memory_space=pltpu.SMEM),
            pl.BlockSpec(memory_space=pltpu.SMEM),
        ],
        out_specs=pl.BlockSpec((1, nh, LANES, LANES), lambda t: (t, 0, 0, 0)),
        out_shape=jax.ShapeDtypeStruct((3, nh, LANES, LANES), F32),
        compiler_params=_cparams("arbitrary"),
        name="rel_bias_band",
    )(bucket, far, rel_bias.astype(F32))


def kernel(x, meta_tokens, attn_norm_g, w_in, forget_bias, rel_bias, w_branch_fox, w_branch_dsa,
           w_out, mlp_norm_g, w_up, w_down, final_norm_g):
    b, L, d = x.shape
    n_meta = meta_tokens.shape[0]
    h_fox = forget_bias.shape[1]
    h_dsa = rel_bias.shape[1]
    w_fox, w_dsa = w_branch_fox.shape[1], w_branch_dsa.shape[1]
    hd = w_fox // h_fox
    w_idx = H_IDX * D_IDX
    assert attn_norm_g.shape[0] == 1, "single-layer block"
    assert w_dsa // h_dsa == hd and hd % LANES == 0
    assert n_meta <= META_PAD and L % CHUNK == 0 and L % META_PAD == 0
    assert MAX_DISTANCE <= LANES and CHUNK <= LANES
    k_top = min(TOPK_MAX, L // 4)

    off_fa = 3 * w_fox
    off_dsa = off_fa + h_fox
    off_qi = off_dsa + 3 * w_dsa
    off_ki = off_qi + w_idx
    off_gate = off_ki + D_IDX + H_IDX
    assert off_gate + 2 * d == w_in.shape[2] and D_IDX + H_IDX + h_fox <= LANES
    wt = jnp.swapaxes(w_in, 1, 2)[0]
    w_fox3 = _columns_bf16(wt, [(0, 3 * w_fox)])
    w_dsa3 = _columns_bf16(wt, [(off_dsa, 3 * w_dsa)])
    w_qi = _columns_bf16(wt, [(off_qi, w_idx)])
    w_gate = _columns_bf16(wt, [(off_gate, 2 * d)])
    w_small = _columns_bf16(wt, [(off_ki, D_IDX + H_IDX), (off_fa, h_fox)])
    wi_lane0, fa_lane0 = D_IDX, D_IDX + H_IDX

    x2 = x.reshape(b * L, d)
    meta = jnp.zeros((META_PAD, d), x.dtype).at[:n_meta].set(meta_tokens.astype(x.dtype))
    u = _rmsnorm(x2, attn_norm_g[0], BF16)
    um = _rmsnorm(meta, attn_norm_g[0], BF16)
    fox3 = _mm(u, w_fox3, BF16).reshape(b, L, -1)
    fox3m = _mm(um, w_fox3, BF16)
    dsa3 = _mm(u, w_dsa3, BF16).reshape(b, L, -1)
    dsa3m = _mm(um, w_dsa3, BF16)
    qi, slab, w_out_b, w_bf_b, w_bd_b = _mm(u, w_qi, BF16, narrow=w_small,
                                            casts=(w_out[0], w_branch_fox[0], w_branch_dsa[0]))
    qi, slab = qi.reshape(b, L, -1), slab.reshape(b, L, LANES)
    gate, w_up_b = _mm(u, w_gate, F32, act="sigmoid", casts=(w_up[0],))
    slabm = _mm(um, w_small, F32, tn=LANES)

    bias_row = jnp.zeros((1, LANES), F32).at[0, fa_lane0:fa_lane0 + h_fox].set(forget_bias[0].astype(F32))
    dt, dtm = _decay(slab, slabm, bias_row, fa_lane0, h_fox, n_meta)
    o_fox = _attention("fox", fox3, fox3m, h_fox, hd, 512, (dt, dtm, n_meta))

    zpad = jnp.zeros(slab.shape[:-1] + (LANES - D_IDX,), BF16)
    ki = slab[..., :D_IDX].astype(BF16)
    kie, kio = jnp.concatenate([ki, zpad], axis=-1), jnp.concatenate([zpad, ki], axis=-1)
    kim = slabm[:, :D_IDX].astype(BF16)
    kiem = jnp.concatenate([kim, zpad[0, :META_PAD]], axis=-1)
    kiom = jnp.concatenate([zpad[0, :META_PAD], kim], axis=-1)
    mask = _indexer_mask(qi, slab, kie, kio, kiem, kiom, wi_lane0, n_meta, k_top)
    band = _band_tables(rel_bias, n_meta)
    o_dsa = _attention("dsa", dsa3, dsa3m, h_dsa, hd, 512, (mask, band))

    mixed = _branch_merge(o_fox.reshape(b * L, w_fox), o_dsa.reshape(b * L, w_dsa),
                          w_bf_b, w_bd_b, gate, BF16)
    h2 = _mm(mixed, w_out_b, F32, res=x2)

    u2 = _rmsnorm(h2, mlp_norm_g[0], BF16)
    a, w_down_b = _mm(u2, w_up_b, BF16, act="relu2", casts=(w_down[0],))
    h3 = _mm_acc(a, w_down_b, h2, F32)
    return _rmsnorm(h3, final_norm_g, x.dtype).reshape(b, L, d)
```

```python
import functools
import math

import jax
import jax.numpy as jnp
from jax import lax
from jax.experimental import pallas as pl
from jax.experimental.pallas import tpu as pltpu

F32 = jnp.float32
BF16 = jnp.bfloat16

CHUNK = 64
H_IDX = 32
D_IDX = 64
TOPK_MAX = 256
MAX_DISTANCE = 128
RMS_EPS = 1e-6

LANES = 128
SUBLANES = 8
BF16_SUBLANES = 16
META_PAD = 128
NEG = -1e30
LOG2E = math.log2(math.e)
INT_MIN = -(2 ** 31)
KEY_NEG_INF = -2139095041
VMEM_LIMIT = 56 * 1024 * 1024
VMEM_LIMIT_BIG = 60 * 1024 * 1024


def _tile(dim, pref):
    if dim <= pref:
        return dim
    for t in range(pref - pref % LANES, 0, -LANES):
        if dim % t == 0:
            return t
    raise ValueError((dim, pref))


def _cparams(*sem, vmem=VMEM_LIMIT):
    return pltpu.CompilerParams(dimension_semantics=sem, vmem_limit_bytes=vmem)


def _rmsnorm_kernel(x_ref, g_ref, o_ref):
    x = x_ref[...].astype(F32)
    ms = jnp.mean(x * x, axis=-1, keepdims=True)
    y = x * lax.rsqrt(ms + RMS_EPS)
    o_ref[...] = (y * g_ref[...]).astype(o_ref.dtype)


def _rmsnorm(x, g, out_dtype):
    m, d = x.shape
    tr = _tile(m, 256)
    return pl.pallas_call(
        _rmsnorm_kernel,
        grid=(m // tr,),
        in_specs=[pl.BlockSpec((tr, d), lambda i: (i, 0)), pl.BlockSpec((1, d), lambda i: (0, 0))],
        out_specs=pl.BlockSpec((tr, d), lambda i: (i, 0)),
        out_shape=jax.ShapeDtypeStruct((m, d), out_dtype),
        compiler_params=_cparams("arbitrary"),
        name="rmsnorm",
    )(x, g.reshape(1, d).astype(F32))


def _columns_kernel(*refs):
    *w_refs, o_ref = refs
    parts = [r[...] for r in w_refs]
    pad = o_ref.shape[1] - sum(p.shape[0] for p in parts)
    if pad:
        parts.append(jnp.zeros((pad, parts[0].shape[1]), parts[0].dtype))
    rows = parts[0] if len(parts) == 1 else jnp.concatenate(parts, axis=0)
    o_ref[...] = rows.T.astype(o_ref.dtype)


def _columns_bf16(wt, ranges, chunk=512):
    n, k = wt.shape
    if len(ranges) == 1 and ranges[0][1] > chunk:
        (lo, width), = ranges
        chunk = _tile(width, chunk)
        steps, widths, out_w = width // chunk, [chunk], chunk
        offs = [lo]
    else:
        steps, widths = 1, [w for _, w in ranges]
        offs = [lo for lo, _ in ranges]
        out_w = -(-sum(widths) // LANES) * LANES
    assert all(o % SUBLANES == 0 and w % SUBLANES == 0 for o, w in zip(offs, widths))
    in_specs = [pl.BlockSpec((pl.Element(w), pl.Element(k)),
                             functools.partial(lambda i, o, w: (pl.multiple_of(o + i * w, SUBLANES), 0), o=o, w=w))
                for o, w in zip(offs, widths)]
    return pl.pallas_call(
        _columns_kernel,
        grid=(steps,),
        in_specs=in_specs,
        out_specs=pl.BlockSpec((k, out_w), lambda i: (0, i)),
        out_shape=jax.ShapeDtypeStruct((k, out_w * steps), BF16),
        compiler_params=_cparams("arbitrary"),
        name="weight_columns",
    )(*([wt] * len(offs)))


def _act(acc, act):
    if act == "sigmoid":
        return 0.5 * jnp.tanh(0.5 * acc) + 0.5
    if act == "relu2":
        r = jnp.maximum(acc, 0.0)
        return r * r
    return acc


def _mm_kernel(*refs, act, has_res, has_narrow, n_cast):
    refs = list(refs)
    a_ref = refs.pop(0)
    b_ref = refs.pop(0)
    r_ref = refs.pop(0) if has_res else None
    bn_ref = refs.pop(0) if has_narrow else None
    cast_in = [refs.pop(0) for _ in range(n_cast)]
    o_ref = refs.pop(0)
    on_ref = refs.pop(0) if has_narrow else None
    cast_out = [refs.pop(0) for _ in range(n_cast)]

    acc = _act(jnp.dot(a_ref[...], b_ref[...], preferred_element_type=F32), act)
    if has_res:
        acc = r_ref[...] + acc
    o_ref[...] = acc.astype(o_ref.dtype)
    if has_narrow:
        @pl.when(pl.program_id(1) == 0)
        def _():
            on_ref[...] = jnp.dot(a_ref[...], bn_ref[...], preferred_element_type=F32).astype(on_ref.dtype)
    for ci, co in zip(cast_in, cast_out):
        co[...] = ci[...].astype(co.dtype)


def _mm(a, b, out_dtype, act=None, res=None, tm=1024, tn=1024, narrow=None, casts=()):
    m, k = a.shape
    _, n = b.shape
    tm, tn = _tile(m, tm), _tile(n, tn)
    ni, nj = m // tm, n // tn
    in_specs = [pl.BlockSpec((tm, k), lambda i, j: (i, 0)), pl.BlockSpec((k, tn), lambda i, j: (0, j))]
    args = [a, b]
    out_specs = [pl.BlockSpec((tm, tn), lambda i, j: (i, j))]
    out_shape = [jax.ShapeDtypeStruct((m, n), out_dtype)]
    if res is not None:
        in_specs.append(pl.BlockSpec((tm, tn), lambda i, j: (i, j)))
        args.append(res)
    if narrow is not None:
        in_specs.append(pl.BlockSpec((k, LANES), lambda i, j: (0, 0)))
        args.append(narrow)
        out_specs.append(pl.BlockSpec((tm, LANES), lambda i, j: (i, 0)))
        out_shape.append(jax.ShapeDtypeStruct((m, LANES), F32))
    riding = []
    for c in casts:
        rows, cols = c.shape
        slab = rows // (ni * nj)
        riding.append(slab * ni * nj == rows and slab % BF16_SUBLANES == 0)
        if riding[-1]:
            spec = pl.BlockSpec((slab, cols), lambda i, j: (i * nj + j, 0))
            in_specs.append(spec)
            args.append(c)
            out_specs.append(spec)
            out_shape.append(jax.ShapeDtypeStruct((rows, cols), BF16))
    outs = list(pl.pallas_call(
        functools.partial(_mm_kernel, act=act, has_res=res is not None, has_narrow=narrow is not None,
                          n_cast=sum(riding)),
        grid=(ni, nj),
        in_specs=in_specs,
        out_specs=out_specs,
        out_shape=out_shape,
        compiler_params=_cparams("arbitrary", "arbitrary", vmem=VMEM_LIMIT_BIG),
        name="matmul",
    )(*args))
    n_main = len(outs) - sum(riding)
    cast_outs = iter(outs[n_main:])
    outs = outs[:n_main] + [next(cast_outs) if r else c.astype(BF16) for r, c in zip(riding, casts)]
    return outs[0] if len(outs) == 1 else tuple(outs)


def _mm_acc_kernel(a_ref, b_ref, r_ref, o_ref):
    kk = pl.program_id(2)
    part = jnp.dot(a_ref[...], b_ref[...], preferred_element_type=F32)

    @pl.when(kk == 0)
    def _():
        o_ref[...] = r_ref[...] + part

    @pl.when(kk > 0)
    def _():
        o_ref[...] += part


def _mm_acc(a, b, res, out_dtype, tm=1024, tn=1024, tk=4096):
    assert out_dtype == F32 and res.dtype == F32
    m, k = a.shape
    _, n = b.shape
    tm, tn, tk = _tile(m, tm), _tile(n, tn), _tile(k, tk)
    return pl.pallas_call(
        _mm_acc_kernel,
        grid=(m // tm, n // tn, k // tk),
        in_specs=[
            pl.BlockSpec((tm, tk), lambda i, j, kk: (i, kk)),
            pl.BlockSpec((tk, tn), lambda i, j, kk: (kk, j)),
            pl.BlockSpec((tm, tn), lambda i, j, kk: (i, j)),
        ],
        out_specs=pl.BlockSpec((tm, tn), lambda i, j, kk: (i, j)),
        out_shape=jax.ShapeDtypeStruct((m, n), out_dtype),
        compiler_params=_cparams("arbitrary", "arbitrary", "arbitrary", vmem=VMEM_LIMIT_BIG),
        name="matmul_acc",
    )(a, b, res)


def _branch_kernel(of_ref, od_ref, wf_ref, wd_ref, g0_ref, g1_ref, o_ref):
    yf = jnp.dot(of_ref[...], wf_ref[...], preferred_element_type=F32)
    yd = jnp.dot(od_ref[...], wd_ref[...], preferred_element_type=F32)
    o_ref[...] = (g0_ref[...] * yf + g1_ref[...] * yd).astype(o_ref.dtype)


def _branch_merge(o_fox, o_dsa, wf, wd, gate, out_dtype, tm=1024, tn=512):
    m, kf = o_fox.shape
    _, kd = o_dsa.shape
    n = wf.shape[1]
    tm, tn = _tile(m, tm), _tile(n, tn)
    nj = n // tn
    return pl.pallas_call(
        _branch_kernel,
        grid=(m // tm, nj),
        in_specs=[
            pl.BlockSpec((tm, kf), lambda i, j: (i, 0)),
            pl.BlockSpec((tm, kd), lambda i, j: (i, 0)),
            pl.BlockSpec((kf, tn), lambda i, j: (0, j)),
            pl.BlockSpec((kd, tn), lambda i, j: (0, j)),
            pl.BlockSpec((tm, tn), lambda i, j: (i, j)),
            pl.BlockSpec((tm, tn), lambda i, j: (i, j + nj)),
        ],
        out_specs=pl.BlockSpec((tm, tn), lambda i, j: (i, j)),
        out_shape=jax.ShapeDtypeStruct((m, n), out_dtype),
        compiler_params=_cparams("arbitrary", "arbitrary"),
        name="branch_merge",
    )(o_fox, o_dsa, wf, wd, gate, gate)


def _split3(x):
    hi = x.astype(BF16)
    r1 = x - hi.astype(F32)
    mid = r1.astype(BF16)
    lo = (r1 - mid.astype(F32)).astype(BF16)
    return hi, mid, lo


def _cumsum_lanes(xt, upper):
    out = None
    for part in _split3(xt):
        d = jnp.dot(part, upper, preferred_element_type=F32)
        out = d if out is None else out + d
    return out


def _decay_kernel(slab_ref, slabm_ref, bias_ref, dt_ref, dtm_ref, *, blk, lane0, nh, n_meta):
    L = slab_ref.shape[1]
    bias = bias_ref[...]

    def upper(n):
        r = lax.broadcasted_iota(jnp.int32, (n, n), 0)
        c = lax.broadcasted_iota(jnp.int32, (n, n), 1)
        return (r <= c).astype(BF16)

    xm = jax.nn.log_sigmoid(slabm_ref[...] + bias)
    dm = _cumsum_lanes(xm.T, upper(META_PAD))
    dtm_ref[0] = dm[lane0:lane0 + nh, :] * -LOG2E
    carry = dm[:, n_meta - 1:n_meta]
    up = upper(blk)
    for i in range(L // blk):
        x = jax.nn.log_sigmoid(slab_ref[0, i * blk:(i + 1) * blk, :] + bias)
        d = _cumsum_lanes(x.T, up) + carry
        dt_ref[0, :, i * blk:(i + 1) * blk] = d[lane0:lane0 + nh, :] * -LOG2E
        carry = d[:, blk - 1:blk]


def _decay(slab, slabm, bias_row, lane0, nh, n_meta):
    b, L, _ = slab.shape
    blk = _tile(L, 256)
    return pl.pallas_call(
        functools.partial(_decay_kernel, blk=blk, lane0=lane0, nh=nh, n_meta=n_meta),
        grid=(b,),
        in_specs=[
            pl.BlockSpec((1, L, LANES), lambda i: (i, 0, 0)),
            pl.BlockSpec((META_PAD, LANES), lambda i: (0, 0)),
            pl.BlockSpec((1, LANES), lambda i: (0, 0)),
        ],
        out_specs=[
            pl.BlockSpec((1, nh, L), lambda i: (i, 0, 0)),
            pl.BlockSpec((1, nh, META_PAD), lambda i: (i, 0, 0)),
        ],
        out_shape=[
            jax.ShapeDtypeStruct((b, nh, L), F32),
            jax.ShapeDtypeStruct((b, nh, META_PAD), F32),
        ],
        compiler_params=_cparams("arbitrary"),
        name="forget_cumsum",
    )(slab, slabm, bias_row)


def _dot_nt(a, b):
    return lax.dot_general(a, b, (((1,), (1,)), ((), ())), preferred_element_type=F32)


def _online_softmax_step(h, hd, s2, v, m_ref, l_ref, acc_ref):
    ts = s2.shape[1]
    m_prev = m_ref[h]
    m_new = jnp.maximum(m_prev, jnp.max(s2, axis=1, keepdims=True))
    alpha = jnp.exp2(m_prev - m_new)
    p = jnp.exp2(s2 - jnp.tile(m_new, (1, ts // LANES))).astype(BF16)
    v_ones = jnp.concatenate([v, jnp.ones((ts, LANES), v.dtype)], axis=1)
    pv = jnp.dot(p, v_ones, preferred_element_type=F32)
    hs = slice(h * hd, (h + 1) * hd)
    acc_ref[:, hs] = acc_ref[:, hs] * jnp.tile(alpha, (1, hd // LANES)) + pv[:, :hd]
    l_ref[h] = alpha * l_ref[h] + pv[:, hd:]
    m_ref[h] = m_new


def _attn_init(m_ref, l_ref, acc_ref):
    m_ref[...] = jnp.full_like(m_ref, NEG)
    l_ref[...] = jnp.zeros_like(l_ref)
    acc_ref[...] = jnp.zeros_like(acc_ref)


def _attn_finish(o_ref, l_ref, acc_ref, nh, hd):
    for h in range(nh):
        hs = slice(h * hd, (h + 1) * hd)
        o_ref[0, :, hs] = (acc_ref[:, hs] / jnp.tile(l_ref[h], (1, hd // LANES))).astype(o_ref.dtype)


def _fox_kernel(qt_ref, kt_ref, q_ref, k_ref, v_ref, km_ref, vm_ref, dt_ref, dtm_ref, o_ref, m_ref, l_ref, acc_ref,
                *, nh, hd, n_meta):
    step = pl.program_id(1)
    qt, kt = qt_ref[step], kt_ref[step]
    tq, ts = q_ref.shape[1], k_ref.shape[1]
    scale = hd ** -0.5 * LOG2E

    @pl.when(kt == 0)
    def _():
        _attn_init(m_ref, l_ref, acc_ref)

    def off_diagonal():
        for h in range(nh):
            hs = slice(h * hd, (h + 1) * hd)
            s = _dot_nt(q_ref[0, :, hs], k_ref[0, :, hs]) * scale + dt_ref[0, h:h + 1, :]
            _online_softmax_step(h, hd, s, v_ref[0, :, hs], m_ref, l_ref, acc_ref)

    def diagonal():
        row = lax.broadcasted_iota(jnp.int32, (tq, ts), 0)
        col = lax.broadcasted_iota(jnp.int32, (tq, ts), 1)
        causal = col <= row
        colm = lax.broadcasted_iota(jnp.int32, (tq, META_PAD), 1)
        pad = jnp.where(colm < n_meta, 0.0, NEG).astype(F32)
        for h in range(nh):
            hs = slice(h * hd, (h + 1) * hd)
            q = q_ref[0, :, hs]
            s = _dot_nt(q, k_ref[0, :, hs]) * scale + dt_ref[0, h:h + 1, :]
            sm = _dot_nt(q, km_ref[:, hs]) * scale + dtm_ref[0, h:h + 1, :] + pad
            s = jnp.concatenate([jnp.where(causal, s, NEG), sm], axis=1)
            v = jnp.concatenate([v_ref[0, :, hs], vm_ref[:, hs]], axis=0)
            _online_softmax_step(h, hd, s, v, m_ref, l_ref, acc_ref)

    @pl.when(kt < qt)
    def _():
        off_diagonal()

    @pl.when(kt == qt)
    def _():
        diagonal()
        _attn_finish(o_ref, l_ref, acc_ref, nh, hd)


def _add_blocks(s, adds):
    rows = []
    for ib in range(s.shape[0] // LANES):
        r = s[ib * LANES:(ib + 1) * LANES]
        if any(i == ib for i, _ in adds):
            r = jnp.concatenate(
                [r[:, jb * LANES:(jb + 1) * LANES] + adds[(ib, jb)] if (ib, jb) in adds
                 else r[:, jb * LANES:(jb + 1) * LANES] for jb in range(s.shape[1] // LANES)], axis=1)
        rows.append(r)
    return jnp.concatenate(rows, axis=0)


def _dsa_kernel(qt_ref, kt_ref, q_ref, k_ref, v_ref, km_ref, vm_ref, mask_ref, maskm_ref, band_ref, o_ref,
                m_ref, l_ref, acc_ref, *, nh, hd):
    step = pl.program_id(1)
    qt, kt = qt_ref[step], kt_ref[step]
    tq, ts = q_ref.shape[1], k_ref.shape[1]
    scale = hd ** -0.5 * LOG2E

    @pl.when(kt == 0)
    def _():
        _attn_init(m_ref, l_ref, acc_ref)

    def off_diagonal(near):
        maskf = mask_ref[0].astype(F32)
        for h in range(nh):
            hs = slice(h * hd, (h + 1) * hd)
            s = _dot_nt(q_ref[0, :, hs], k_ref[0, :, hs]) * scale + maskf
            if near:
                s = _add_blocks(s, {(0, ts // LANES - 1): band_ref[1, h]})
            _online_softmax_step(h, hd, s, v_ref[0, :, hs], m_ref, l_ref, acc_ref)

    def diagonal():
        first = (qt == 0).astype(F32)
        maskf = mask_ref[0].astype(F32)
        maskm = maskm_ref[0].astype(F32)
        for h in range(nh):
            hs = slice(h * hd, (h + 1) * hd)
            q = q_ref[0, :, hs]
            s = _dot_nt(q, k_ref[0, :, hs]) * scale + maskf
            sm = _dot_nt(q, km_ref[:, hs]) * scale + maskm
            s = jnp.concatenate([s, sm], axis=1)
            adds = {(0, ts // LANES): band_ref[2, h] * first}
            for ib in range(tq // LANES):
                adds[(ib, ib)] = band_ref[0, h]
                if ib > 0:
                    adds[(ib, ib - 1)] = band_ref[1, h]
            s = _add_blocks(s, adds)
            v = jnp.concatenate([v_ref[0, :, hs], vm_ref[:, hs]], axis=0)
            _online_softmax_step(h, hd, s, v, m_ref, l_ref, acc_ref)

    @pl.when(kt < qt - 1)
    def _():
        off_diagonal(False)

    @pl.when(kt == qt - 1)
    def _():
        off_diagonal(True)

    @pl.when(kt == qt)
    def _():
        diagonal()
        _attn_finish(o_ref, l_ref, acc_ref, nh, hd)


def _attention(kind, big, bigm, nh, hd, tile, extra):
    b, L, _ = big.shape
    w = nh * hd
    q_blk, k_blk, v_blk = 0, 1, 2
    t = _tile(L, tile)
    nq = L // t
    pairs = [(qi, ki) for qi in range(nq) for ki in range(qi + 1)]
    qt_tab = jnp.asarray([p[0] for p in pairs], jnp.int32)
    kt_tab = jnp.asarray([p[1] for p in pairs], jnp.int32)
    q_spec = pl.BlockSpec((1, t, w), lambda bi, s, qt, kt: (bi, qt[s], q_blk))
    kv = lambda blk: pl.BlockSpec((1, t, w), lambda bi, s, qt, kt: (bi, kt[s], blk))
    kvm = lambda blk: pl.BlockSpec((META_PAD, w), lambda bi, s, qt, kt: (0, blk))
    scratch = [
        pltpu.VMEM((nh, t, LANES), F32),
        pltpu.VMEM((nh, t, LANES), F32),
        pltpu.VMEM((t, w), F32),
    ]
    if kind == "fox":
        dt, dtm, n_meta = extra
        body = functools.partial(_fox_kernel, nh=nh, hd=hd, n_meta=n_meta)
        in_specs = [q_spec, kv(k_blk), kv(v_blk), kvm(k_blk), kvm(v_blk),
                    pl.BlockSpec((1, nh, t), lambda bi, s, qt, kt: (bi, 0, kt[s])),
                    pl.BlockSpec((1, nh, META_PAD), lambda bi, s, qt, kt: (bi, 0, 0))]
        args = [big, big, big, bigm, bigm, dt, dtm]
    else:
        mask, band = extra
        body = functools.partial(_dsa_kernel, nh=nh, hd=hd)
        in_specs = [q_spec, kv(k_blk), kv(v_blk), kvm(k_blk), kvm(v_blk),
                    pl.BlockSpec((1, t, t), lambda bi, s, qt, kt: (bi, qt[s], kt[s])),
                    pl.BlockSpec((1, t, META_PAD), lambda bi, s, qt, kt: (bi, qt[s], L // META_PAD)),
                    pl.BlockSpec((3, nh, LANES, LANES), lambda bi, s, qt, kt: (0, 0, 0, 0))]
        args = [big, big, big, bigm, bigm, mask, mask, band]
    return pl.pallas_call(
        body,
        grid_spec=pltpu.PrefetchScalarGridSpec(
            num_scalar_prefetch=2,
            grid=(b, len(pairs)),
            in_specs=in_specs,
            out_specs=pl.BlockSpec((1, t, w), lambda bi, s, qt, kt: (bi, qt[s], 0)),
            scratch_shapes=scratch,
        ),
        out_shape=jax.ShapeDtypeStruct((b, L, w), BF16),
        compiler_params=_cparams("arbitrary", "arbitrary"),
        name=kind + "_attention",
    )(qt_tab, kt_tab, *args)


def _sortable(x):
    bits = pltpu.bitcast(x, jnp.int32)
    return bits ^ ((bits >> 31) & 0x7FFFFFFF)


def _index_kernel(qi_ref, w_ref, kie_ref, kio_ref, kiem_ref, kiom_ref, mask_ref, wb_ref, st_ref, smt_ref,
                  *, tc, w_lane0, n_meta, k_top):
    qt = pl.program_id(1)
    tq = qi_ref.shape[1]
    L = kie_ref.shape[1]
    n_ct = (qt + 1) * tq // tc
    npair = H_IDX // 2
    wscale = (H_IDX ** -0.5) * (D_IDX ** -0.5)

    wv = w_ref[0]
    for h in range(H_IDX):
        col = wv[:, w_lane0 + h:w_lane0 + h + 1] * wscale
        wb_ref[h] = jnp.broadcast_to(col, (tq, LANES))

    def score_tile(ke, ko):
        n = ke.shape[0]
        accs = [jnp.zeros((tq, LANES), F32) for _ in range(n // LANES)]
        for j in range(npair):
            lhs = qi_ref[0, :, j * LANES:(j + 1) * LANES]
            de = jnp.maximum(_dot_nt(lhs, ke), 0.0)
            do = jnp.maximum(_dot_nt(lhs, ko), 0.0)
            for c in range(n // LANES):
                cs = slice(c * LANES, (c + 1) * LANES)
                accs[c] = accs[c] + wb_ref[2 * j] * de[:, cs] + wb_ref[2 * j + 1] * do[:, cs]
        return accs[0] if len(accs) == 1 else jnp.concatenate(accs, axis=1)

    colm = lax.broadcasted_iota(jnp.int32, (tq, META_PAD), 1)
    sm = jnp.where(colm < n_meta, score_tile(kiem_ref[...], kiom_ref[...]), -jnp.inf)
    smt_ref[...] = _sortable(sm.T)

    qchunk = (qt * tq + lax.broadcasted_iota(jnp.int32, (tq, tc), 0)) // CHUNK

    def tile_body(c, carry):
        start = pl.multiple_of(c * tc, tc)
        sc = score_tile(kie_ref[0, pl.ds(start, tc), :], kio_ref[0, pl.ds(start, tc), :])
        kchunk = (start + lax.broadcasted_iota(jnp.int32, (tq, tc), 1)) // CHUNK
        sc = jnp.where(kchunk <= qchunk, sc, -jnp.inf)
        st_ref[c] = _sortable(sc.T)
        return carry

    lax.fori_loop(0, n_ct, tile_body, 0)

    sub = 8
    nacc = 4

    def count_tile(tile, thr8, accs):
        for r in range(tile.shape[0] // sub):
            accs[r % nacc] = accs[r % nacc] + (tile[r * sub:(r + 1) * sub, :] >= thr8).astype(jnp.int32)
        return accs

    def count_ge(thr8):
        accs = count_tile(smt_ref[...], thr8, [jnp.zeros((sub, tq), jnp.int32) for _ in range(nacc)])
        accs = lax.fori_loop(0, n_ct, lambda c, a: tuple(count_tile(st_ref[c], thr8, list(a))), tuple(accs))
        total = functools.reduce(lambda x, y: x + y, accs)
        return jnp.broadcast_to(jnp.sum(total, axis=0, keepdims=True), (sub, tq))

    def bit_body(it, ans_u):
        trial_u = ans_u | (jnp.int32(1) << (31 - it))
        cnt = count_ge(trial_u ^ INT_MIN)
        return jnp.where(cnt >= k_top, trial_u, ans_u)

    ans_u = lax.fori_loop(0, 32, bit_body, jnp.zeros((sub, tq), jnp.int32))
    thr8 = jnp.maximum(ans_u ^ INT_MIN, KEY_NEG_INF + 1)
    thr = thr8[:1]

    cnt_ge = count_ge(thr8)
    tied = jnp.max(((cnt_ge > k_top) & (thr8 > KEY_NEG_INF + 1)).astype(jnp.int32)) > 0

    @pl.when(tied)
    def _():
        need = k_top - count_ge(thr8 + 1)[:1]
        posm = lax.broadcasted_iota(jnp.int32, (META_PAD, tq), 0)
        posf = n_meta + lax.broadcasted_iota(jnp.int32, (tc, tq), 0)

        def tied_before(cut):
            acc = jnp.sum(((smt_ref[...] == thr) & (posm < cut)).astype(jnp.int32), axis=0, keepdims=True)

            def body(c, acc):
                hit = (st_ref[c] == thr) & (posf + c * tc < cut)
                return acc + jnp.sum(hit.astype(jnp.int32), axis=0, keepdims=True)

            return lax.fori_loop(0, n_ct, body, acc)

        nbits = (L + META_PAD).bit_length()

        def pos_body(it, cut):
            trial = cut | (jnp.int32(1) << (nbits - 1 - it))
            return jnp.where(tied_before(trial) < need, trial, cut)

        cut = lax.fori_loop(0, nbits, pos_body, jnp.zeros((1, tq), jnp.int32))
        smt_ref[...] = smt_ref[...] - ((smt_ref[...] == thr) & (posm > cut)).astype(jnp.int32)

        def lower(c, carry):
            st_ref[c] = st_ref[c] - ((st_ref[c] == thr) & (posf + c * tc > cut)).astype(jnp.int32)
            return carry

        lax.fori_loop(0, n_ct, lower, 0)

    def mask_of(keys_t):
        return jnp.where(keys_t >= thr, 0.0, NEG).astype(F32).T.astype(mask_ref.dtype)

    mask_ref[0, :, L:] = mask_of(smt_ref[...])
    for c in range(L // tc):
        cols = slice(c * tc, (c + 1) * tc)

        @pl.when(c < n_ct)
        def _():
            mask_ref[0, :, cols] = mask_of(st_ref[c])

        @pl.when(c >= n_ct)
        def _():
            mask_ref[0, :, cols] = jnp.full((tq, tc), NEG, mask_ref.dtype)


def _indexer_mask(qi, slab, kie, kio, kiem, kiom, w_lane0, n_meta, k_top, tq=256, tc=256):
    b, L, _ = qi.shape
    tq = _tile(L, tq)
    tc = _tile(tq, tc)
    wq = H_IDX * D_IDX
    return pl.pallas_call(
        functools.partial(_index_kernel, tc=tc, w_lane0=w_lane0, n_meta=n_meta, k_top=k_top),
        grid=(b, L // tq),
        in_specs=[
            pl.BlockSpec((1, tq, wq), lambda bi, qi: (bi, qi, 0)),
            pl.BlockSpec((1, tq, LANES), lambda bi, qi: (bi, qi, 0)),
            pl.BlockSpec((1, L, LANES), lambda bi, qi: (bi, 0, 0)),
            pl.BlockSpec((1, L, LANES), lambda bi, qi: (bi, 0, 0)),
            pl.BlockSpec((META_PAD, LANES), lambda bi, qi: (0, 0)),
            pl.BlockSpec((META_PAD, LANES), lambda bi, qi: (0, 0)),
        ],
        out_specs=pl.BlockSpec((1, tq, L + META_PAD), lambda bi, qi: (bi, qi, 0)),
        out_shape=jax.ShapeDtypeStruct((b, L, L + META_PAD), BF16),
        scratch_shapes=[
            pltpu.VMEM((H_IDX, tq, LANES), F32),
            pltpu.VMEM((L // tc, tc, tq), jnp.int32),
            pltpu.VMEM((META_PAD, tq), jnp.int32),
        ],
        compiler_params=_cparams("arbitrary", "arbitrary"),
        name="indexer_topk_mask",
    )(qi, slab, kie, kio, kiem, kiom)


def _t5_bucket(rel, n_buckets):
    half = n_buckets // 2
    max_exact = half // 2
    ret = jnp.where(rel > 0, half, 0)
    n = jnp.abs(rel)
    nf = jnp.maximum(n, 1).astype(jnp.float32)
    large = max_exact + (jnp.log(nf / max_exact) / math.log(MAX_DISTANCE / max_exact)
                         * (half - max_exact)).astype(jnp.int32)
    large = jnp.minimum(large, half - 1)
    return ret + jnp.where(n < max_exact, n, large)


def _band_kernel(bucket_ref, far_ref, rb_ref, o_ref, *, n_buckets, nh):
    bucket = bucket_ref[0]
    far = far_ref[0]
    for h in range(nh):
        acc = jnp.zeros(bucket.shape, F32)
        for bk in range(n_buckets):
            acc = jnp.where(bucket == bk, rb_ref[bk, h], acc)
        o_ref[0, h] = (acc - rb_ref[far, h]) * LOG2E


def _band_tables(rel_bias, n_meta):
    n_buckets, nh = rel_bias.shape
    i = jnp.arange(LANES, dtype=jnp.int32)[:, None]
    j = jnp.arange(LANES, dtype=jnp.int32)[None, :]
    rel = jnp.stack([j - i, j - i - LANES, j - n_meta - i])
    bucket = _t5_bucket(rel, n_buckets).astype(jnp.int32)
    far = _t5_bucket(jnp.full((1,), -MAX_DISTANCE, jnp.int32), n_buckets).astype(jnp.int32)
    return pl.pallas_call(
        functools.partial(_band_kernel, n_buckets=n_buckets, nh=nh),
        grid=(3,),
        in_specs=[
            pl.BlockSpec((1, LANES, LANES), lambda t: (t, 0, 0)),
            pl.BlockSpec(memory_space=pltpu.SMEM),
            pl.BlockSpec(memory_space=pltpu.SMEM),
        ],
        out_specs=pl.BlockSpec((1, nh, LANES, LANES), lambda t: (t, 0, 0, 0)),
        out_shape=jax.ShapeDtypeStruct((3, nh, LANES, LANES), F32),
        compiler_params=_cparams("arbitrary"),
        name="rel_bias_band",
    )(bucket, far, rel_bias.astype(F32))


def kernel(x, meta_tokens, attn_norm_g, w_in, forget_bias, rel_bias, w_branch_fox, w_branch_dsa,
           w_out, mlp_norm_g, w_up, w_down, final_norm_g):
    b, L, d = x.shape
    n_meta = meta_tokens.shape[0]
    h_fox = forget_bias.shape[1]
    h_dsa = rel_bias.shape[1]
    w_fox, w_dsa = w_branch_fox.shape[1], w_branch_dsa.shape[1]
    hd = w_fox // h_fox
    w_idx = H_IDX * D_IDX
    assert attn_norm_g.shape[0] == 1, "single-layer block"
    assert w_dsa // h_dsa == hd and hd % LANES == 0
    assert n_meta <= META_PAD and L % CHUNK == 0 and L % META_PAD == 0
    assert MAX_DISTANCE <= LANES and LANES % CHUNK == 0
    k_top = min(TOPK_MAX, L // 4)

    off_fa = 3 * w_fox
    off_dsa = off_fa + h_fox
    off_qi = off_dsa + 3 * w_dsa
    off_ki = off_qi + w_idx
    off_gate = off_ki + D_IDX + H_IDX
    assert off_gate + 2 * d == w_in.shape[2] and D_IDX + H_IDX + h_fox <= LANES
    wt = jnp.swapaxes(w_in, 1, 2)[0]
    w_fox3 = _columns_bf16(wt, [(0, 3 * w_fox)])
    w_dsa3 = _columns_bf16(wt, [(off_dsa, 3 * w_dsa)])
    w_qi = _columns_bf16(wt, [(off_qi, w_idx)])
    w_gate = _columns_bf16(wt, [(off_gate, 2 * d)])
    w_small = _columns_bf16(wt, [(off_ki, D_IDX + H_IDX), (off_fa, h_fox)])
    wi_lane0, fa_lane0 = D_IDX, D_IDX + H_IDX

    x2 = x.reshape(b * L, d)
    meta = jnp.zeros((META_PAD, d), x.dtype).at[:n_meta].set(meta_tokens.astype(x.dtype))
    u = _rmsnorm(x2, attn_norm_g[0], BF16)
    um = _rmsnorm(meta, attn_norm_g[0], BF16)
    fox3 = _mm(u, w_fox3, BF16).reshape(b, L, -1)
    fox3m = _mm(um, w_fox3, BF16)
    dsa3 = _mm(u, w_dsa3, BF16).reshape(b, L, -1)
    dsa3m = _mm(um, w_dsa3, BF16)
    qi, slab, w_out_b, w_bf_b, w_bd_b = _mm(u, w_qi, BF16, narrow=w_small, tn=512,
                                            casts=(w_out[0], w_branch_fox[0], w_branch_dsa[0]))
    qi, slab = qi.reshape(b, L, -1), slab.reshape(b, L, LANES)
    gate, w_up_b = _mm(u, w_gate, F32, act="sigmoid", casts=(w_up[0],))
    slabm = _mm(um, w_small, F32, tn=LANES)

    bias_row = jnp.zeros((1, LANES), F32).at[0, fa_lane0:fa_lane0 + h_fox].set(forget_bias[0].astype(F32))
    dt, dtm = _decay(slab, slabm, bias_row, fa_lane0, h_fox, n_meta)
    o_fox = _attention("fox", fox3, fox3m, h_fox, hd, 512, (dt, dtm, n_meta))

    zpad = jnp.zeros(slab.shape[:-1] + (LANES - D_IDX,), BF16)
    ki = slab[..., :D_IDX].astype(BF16)
    kie, kio = jnp.concatenate([ki, zpad], axis=-1), jnp.concatenate([zpad, ki], axis=-1)
    kim = slabm[:, :D_IDX].astype(BF16)
    kiem = jnp.concatenate([kim, zpad[0, :META_PAD]], axis=-1)
    kiom = jnp.concatenate([zpad[0, :META_PAD], kim], axis=-1)
    mask = _indexer_mask(qi, slab, kie, kio, kiem, kiom, wi_lane0, n_meta, k_top)
    band = _band_tables(rel_bias, n_meta)
    o_dsa = _attention("dsa", dsa3, dsa3m, h_dsa, hd, 512, (mask, band))

    mixed = _branch_merge(o_fox.reshape(b * L, w_fox), o_dsa.reshape(b * L, w_dsa),
                          w_bf_b, w_bd_b, gate, BF16)
    h2 = _mm(mixed, w_out_b, F32, res=x2, tn=512)

    u2 = _rmsnorm(h2, mlp_norm_g[0], BF16)
    a, w_down_b = _mm(u2, w_up_b, BF16, act="relu2", casts=(w_down[0],))
    h3 = _mm_acc(a, w_down_b, h2, F32)
    return _rmsnorm(h3, final_norm_g, x.dtype).reshape(b, L, d)
```

```python
import functools
import math

import jax
import jax.numpy as jnp
from jax import lax
from jax.experimental import pallas as pl
from jax.experimental.pallas import tpu as pltpu

F32 = jnp.float32
BF16 = jnp.bfloat16

CHUNK = 64
H_IDX = 32
D_IDX = 64
TOPK_MAX = 256
MAX_DISTANCE = 128
RMS_EPS = 1e-6

LANES = 128
SUBLANES = 8
BF16_SUBLANES = 16
META_PAD = 128
NEG = -1e30
LOG2E = math.log2(math.e)
INT_MIN = -(2 ** 31)
KEY_NEG_INF = -2139095041
VMEM_LIMIT = 56 * 1024 * 1024
VMEM_LIMIT_BIG = 60 * 1024 * 1024


def _tile(dim, pref):
    if dim <= pref:
        return dim
    for t in range(pref - pref % LANES, 0, -LANES):
        if dim % t == 0:
            return t
    raise ValueError((dim, pref))


def _cparams(*sem, vmem=VMEM_LIMIT):
    return pltpu.CompilerParams(dimension_semantics=sem, vmem_limit_bytes=vmem)


def _rmsnorm_kernel(x_ref, g_ref, o_ref):
    x = x_ref[...].astype(F32)
    ms = jnp.mean(x * x, axis=-1, keepdims=True)
    y = x * lax.rsqrt(ms + RMS_EPS)
    o_ref[...] = (y * g_ref[...]).astype(o_ref.dtype)


def _rmsnorm(x, g, out_dtype):
    m, d = x.shape
    tr = _tile(m, 256)
    return pl.pallas_call(
        _rmsnorm_kernel,
        grid=(m // tr,),
        in_specs=[pl.BlockSpec((tr, d), lambda i: (i, 0)), pl.BlockSpec((1, d), lambda i: (0, 0))],
        out_specs=pl.BlockSpec((tr, d), lambda i: (i, 0)),
        out_shape=jax.ShapeDtypeStruct((m, d), out_dtype),
        compiler_params=_cparams("arbitrary"),
        name="rmsnorm",
    )(x, g.reshape(1, d).astype(F32))


def _columns_kernel(*refs):
    *w_refs, o_ref = refs
    parts = [r[...] for r in w_refs]
    pad = o_ref.shape[1] - sum(p.shape[0] for p in parts)
    if pad:
        parts.append(jnp.zeros((pad, parts[0].shape[1]), parts[0].dtype))
    rows = parts[0] if len(parts) == 1 else jnp.concatenate(parts, axis=0)
    o_ref[...] = rows.T.astype(o_ref.dtype)


def _columns_bf16(wt, ranges, chunk=512):
    n, k = wt.shape
    if len(ranges) == 1 and ranges[0][1] > chunk:
        (lo, width), = ranges
        chunk = _tile(width, chunk)
        steps, widths, out_w = width // chunk, [chunk], chunk
        offs = [lo]
    else:
        steps, widths = 1, [w for _, w in ranges]
        offs = [lo for lo, _ in ranges]
        out_w = -(-sum(widths) // LANES) * LANES
    assert all(o % SUBLANES == 0 and w % SUBLANES == 0 for o, w in zip(offs, widths))
    in_specs = [pl.BlockSpec((pl.Element(w), pl.Element(k)),
                             functools.partial(lambda i, o, w: (pl.multiple_of(o + i * w, SUBLANES), 0), o=o, w=w))
                for o, w in zip(offs, widths)]
    return pl.pallas_call(
        _columns_kernel,
        grid=(steps,),
        in_specs=in_specs,
        out_specs=pl.BlockSpec((k, out_w), lambda i: (0, i)),
        out_shape=jax.ShapeDtypeStruct((k, out_w * steps), BF16),
        compiler_params=_cparams("arbitrary"),
        name="weight_columns",
    )(*([wt] * len(offs)))


def _act(acc, act):
    if act == "sigmoid":
        return 0.5 * jnp.tanh(0.5 * acc) + 0.5
    if act == "relu2":
        r = jnp.maximum(acc, 0.0)
        return r * r
    return acc


def _mm_kernel(*refs, act, has_res, has_narrow, n_cast):
    refs = list(refs)
    a_ref = refs.pop(0)
    b_ref = refs.pop(0)
    r_ref = refs.pop(0) if has_res else None
    bn_ref = refs.pop(0) if has_narrow else None
    cast_in = [refs.pop(0) for _ in range(n_cast)]
    o_ref = refs.pop(0)
    on_ref = refs.pop(0) if has_narrow else None
    cast_out = [refs.pop(0) for _ in range(n_cast)]

    acc = _act(jnp.dot(a_ref[...], b_ref[...], preferred_element_type=F32), act)
    if has_res:
        acc = r_ref[...] + acc
    o_ref[...] = acc.astype(o_ref.dtype)
    if has_narrow:
        @pl.when(pl.program_id(1) == 0)
        def _():
            on_ref[...] = jnp.dot(a_ref[...], bn_ref[...], preferred_element_type=F32).astype(on_ref.dtype)
    for ci, co in zip(cast_in, cast_out):
        co[...] = ci[...].astype(co.dtype)


def _mm(a, b, out_dtype, act=None, res=None, tm=1024, tn=1024, narrow=None, casts=()):
    m, k = a.shape
    _, n = b.shape
    tm, tn = _tile(m, tm), _tile(n, tn)
    ni, nj = m // tm, n // tn
    in_specs = [pl.BlockSpec((tm, k), lambda i, j: (i, 0)), pl.BlockSpec((k, tn), lambda i, j: (0, j))]
    args = [a, b]
    out_specs = [pl.BlockSpec((tm, tn), lambda i, j: (i, j))]
    out_shape = [jax.ShapeDtypeStruct((m, n), out_dtype)]
    if res is not None:
        in_specs.append(pl.BlockSpec((tm, tn), lambda i, j: (i, j)))
        args.append(res)
    if narrow is not None:
        in_specs.append(pl.BlockSpec((k, LANES), lambda i, j: (0, 0)))
        args.append(narrow)
        out_specs.append(pl.BlockSpec((tm, LANES), lambda i, j: (i, 0)))
        out_shape.append(jax.ShapeDtypeStruct((m, LANES), F32))
    riding = []
    for c in casts:
        rows, cols = c.shape
        slab = rows // (ni * nj)
        riding.append(slab * ni * nj == rows and slab % BF16_SUBLANES == 0)
        if riding[-1]:
            spec = pl.BlockSpec((slab, cols), lambda i, j: (i * nj + j, 0))
            in_specs.append(spec)
            args.append(c)
            out_specs.append(spec)
            out_shape.append(jax.ShapeDtypeStruct((rows, cols), BF16))
    outs = list(pl.pallas_call(
        functools.partial(_mm_kernel, act=act, has_res=res is not None, has_narrow=narrow is not None,
                          n_cast=sum(riding)),
        grid=(ni, nj),
        in_specs=in_specs,
        out_specs=out_specs,
        out_shape=out_shape,
        compiler_params=_cparams("arbitrary", "arbitrary", vmem=VMEM_LIMIT_BIG),
        name="matmul",
    )(*args))
    n_main = len(outs) - sum(riding)
    cast_outs = iter(outs[n_main:])
    outs = outs[:n_main] + [next(cast_outs) if r else c.astype(BF16) for r, c in zip(riding, casts)]
    return outs[0] if len(outs) == 1 else tuple(outs)


def _mm_acc_kernel(a_ref, b_ref, r_ref, o_ref):
    kk = pl.program_id(2)
    part = jnp.dot(a_ref[...], b_ref[...], preferred_element_type=F32)

    @pl.when(kk == 0)
    def _():
        o_ref[...] = r_ref[...] + part

    @pl.when(kk > 0)
    def _():
        o_ref[...] += part


def _mm_acc(a, b, res, out_dtype, tm=1024, tn=1024, tk=4096):
    assert out_dtype == F32 and res.dtype == F32
    m, k = a.shape
    _, n = b.shape
    tm, tn, tk = _tile(m, tm), _tile(n, tn), _tile(k, tk)
    return pl.pallas_call(
        _mm_acc_kernel,
        grid=(m // tm, n // tn, k // tk),
        in_specs=[
            pl.BlockSpec((tm, tk), lambda i, j, kk: (i, kk)),
            pl.BlockSpec((tk, tn), lambda i, j, kk: (kk, j)),
            pl.BlockSpec((tm, tn), lambda i, j, kk: (i, j)),
        ],
        out_specs=pl.BlockSpec((tm, tn), lambda i, j, kk: (i, j)),
        out_shape=jax.ShapeDtypeStruct((m, n), out_dtype),
        compiler_params=_cparams("arbitrary", "arbitrary", "arbitrary", vmem=VMEM_LIMIT_BIG),
        name="matmul_acc",
    )(a, b, res)


def _branch_kernel(of_ref, od_ref, wf_ref, wd_ref, g0_ref, g1_ref, o_ref):
    yf = jnp.dot(of_ref[...], wf_ref[...], preferred_element_type=F32)
    yd = jnp.dot(od_ref[...], wd_ref[...], preferred_element_type=F32)
    o_ref[...] = (g0_ref[...] * yf + g1_ref[...] * yd).astype(o_ref.dtype)


def _branch_merge(o_fox, o_dsa, wf, wd, gate, out_dtype, tm=1024, tn=512):
    m, kf = o_fox.shape
    _, kd = o_dsa.shape
    n = wf.shape[1]
    tm, tn = _tile(m, tm), _tile(n, tn)
    nj = n // tn
    return pl.pallas_call(
        _branch_kernel,
        grid=(m // tm, nj),
        in_specs=[
            pl.BlockSpec((tm, kf), lambda i, j: (i, 0)),
            pl.BlockSpec((tm, kd), lambda i, j: (i, 0)),
            pl.BlockSpec((kf, tn), lambda i, j: (0, j)),
            pl.BlockSpec((kd, tn), lambda i, j: (0, j)),
            pl.BlockSpec((tm, tn), lambda i, j: (i, j)),
            pl.BlockSpec((tm, tn), lambda i, j: (i, j + nj)),
        ],
        out_specs=pl.BlockSpec((tm, tn), lambda i, j: (i, j)),
        out_shape=jax.ShapeDtypeStruct((m, n), out_dtype),
        compiler_params=_cparams("arbitrary", "arbitrary"),
        name="branch_merge",
    )(o_fox, o_dsa, wf, wd, gate, gate)


def _split3(x):
    hi = x.astype(BF16)
    r1 = x - hi.astype(F32)
    mid = r1.astype(BF16)
    lo = (r1 - mid.astype(F32)).astype(BF16)
    return hi, mid, lo


def _cumsum_lanes(xt, upper):
    out = None
    for part in _split3(xt):
        d = jnp.dot(part, upper, preferred_element_type=F32)
        out = d if out is None else out + d
    return out


def _decay_kernel(slab_ref, slabm_ref, bias_ref, dt_ref, dtm_ref, *, blk, lane0, nh, n_meta):
    L = slab_ref.shape[1]
    bias = bias_ref[...]

    def upper(n):
        r = lax.broadcasted_iota(jnp.int32, (n, n), 0)
        c = lax.broadcasted_iota(jnp.int32, (n, n), 1)
        return (r <= c).astype(BF16)

    xm = jax.nn.log_sigmoid(slabm_ref[...] + bias)
    dm = _cumsum_lanes(xm.T, upper(META_PAD))
    dtm_ref[0] = dm[lane0:lane0 + nh, :] * -LOG2E
    carry = dm[:, n_meta - 1:n_meta]
    up = upper(blk)
    for i in range(L // blk):
        x = jax.nn.log_sigmoid(slab_ref[0, i * blk:(i + 1) * blk, :] + bias)
        d = _cumsum_lanes(x.T, up) + carry
        dt_ref[0, :, i * blk:(i + 1) * blk] = d[lane0:lane0 + nh, :] * -LOG2E
        carry = d[:, blk - 1:blk]


def _decay(slab, slabm, bias_row, lane0, nh, n_meta):
    b, L, _ = slab.shape
    blk = _tile(L, 256)
    return pl.pallas_call(
        functools.partial(_decay_kernel, blk=blk, lane0=lane0, nh=nh, n_meta=n_meta),
        grid=(b,),
        in_specs=[
            pl.BlockSpec((1, L, LANES), lambda i: (i, 0, 0)),
            pl.BlockSpec((META_PAD, LANES), lambda i: (0, 0)),
            pl.BlockSpec((1, LANES), lambda i: (0, 0)),
        ],
        out_specs=[
            pl.BlockSpec((1, nh, L), lambda i: (i, 0, 0)),
            pl.BlockSpec((1, nh, META_PAD), lambda i: (i, 0, 0)),
        ],
        out_shape=[
            jax.ShapeDtypeStruct((b, nh, L), F32),
            jax.ShapeDtypeStruct((b, nh, META_PAD), F32),
        ],
        compiler_params=_cparams("arbitrary"),
        name="forget_cumsum",
    )(slab, slabm, bias_row)


def _dot_nt(a, b):
    return lax.dot_general(a, b, (((1,), (1,)), ((), ())), preferred_element_type=F32)


def _online_softmax_step(h, hd, s2, v, m_ref, l_ref, acc_ref):
    ts = s2.shape[1]
    m_prev = m_ref[h]
    m_new = jnp.maximum(m_prev, jnp.max(s2, axis=1, keepdims=True))
    alpha = jnp.exp2(m_prev - m_new)
    p = jnp.exp2(s2 - jnp.tile(m_new, (1, ts // LANES))).astype(BF16)
    v_ones = jnp.concatenate([v, jnp.ones((ts, LANES), v.dtype)], axis=1)
    pv = jnp.dot(p, v_ones, preferred_element_type=F32)
    hs = slice(h * hd, (h + 1) * hd)
    acc_ref[:, hs] = acc_ref[:, hs] * jnp.tile(alpha, (1, hd // LANES)) + pv[:, :hd]
    l_ref[h] = alpha * l_ref[h] + pv[:, hd:]
    m_ref[h] = m_new


def _attn_init(m_ref, l_ref, acc_ref):
    m_ref[...] = jnp.full_like(m_ref, NEG)
    l_ref[...] = jnp.zeros_like(l_ref)
    acc_ref[...] = jnp.zeros_like(acc_ref)


def _attn_finish(o_ref, l_ref, acc_ref, nh, hd):
    for h in range(nh):
        hs = slice(h * hd, (h + 1) * hd)
        o_ref[0, :, hs] = (acc_ref[:, hs] / jnp.tile(l_ref[h], (1, hd // LANES))).astype(o_ref.dtype)


def _fox_kernel(qt_ref, kt_ref, q_ref, k_ref, v_ref, km_ref, vm_ref, dt_ref, dtm_ref, o_ref, m_ref, l_ref, acc_ref,
                *, nh, hd, n_meta):
    step = pl.program_id(1)
    qt, kt = qt_ref[step], kt_ref[step]
    tq, ts = q_ref.shape[1], k_ref.shape[1]
    scale = hd ** -0.5 * LOG2E

    @pl.when(kt == 0)
    def _():
        _attn_init(m_ref, l_ref, acc_ref)

    def off_diagonal():
        for h in range(nh):
            hs = slice(h * hd, (h + 1) * hd)
            s = _dot_nt(q_ref[0, :, hs], k_ref[0, :, hs]) * scale + dt_ref[0, h:h + 1, :]
            _online_softmax_step(h, hd, s, v_ref[0, :, hs], m_ref, l_ref, acc_ref)

    def diagonal():
        row = lax.broadcasted_iota(jnp.int32, (tq, ts), 0)
        col = lax.broadcasted_iota(jnp.int32, (tq, ts), 1)
        causal = col <= row
        colm = lax.broadcasted_iota(jnp.int32, (tq, META_PAD), 1)
        pad = jnp.where(colm < n_meta, 0.0, NEG).astype(F32)
        for h in range(nh):
            hs = slice(h * hd, (h + 1) * hd)
            q = q_ref[0, :, hs]
            s = _dot_nt(q, k_ref[0, :, hs]) * scale + dt_ref[0, h:h + 1, :]
            sm = _dot_nt(q, km_ref[:, hs]) * scale + dtm_ref[0, h:h + 1, :] + pad
            s = jnp.concatenate([jnp.where(causal, s, NEG), sm], axis=1)
            v = jnp.concatenate([v_ref[0, :, hs], vm_ref[:, hs]], axis=0)
            _online_softmax_step(h, hd, s, v, m_ref, l_ref, acc_ref)

    @pl.when(kt < qt)
    def _():
        off_diagonal()

    @pl.when(kt == qt)
    def _():
        diagonal()
        _attn_finish(o_ref, l_ref, acc_ref, nh, hd)


def _add_blocks(s, adds):
    rows = []
    for ib in range(s.shape[0] // LANES):
        r = s[ib * LANES:(ib + 1) * LANES]
        if any(i == ib for i, _ in adds):
            r = jnp.concatenate(
                [r[:, jb * LANES:(jb + 1) * LANES] + adds[(ib, jb)] if (ib, jb) in adds
                 else r[:, jb * LANES:(jb + 1) * LANES] for jb in range(s.shape[1] // LANES)], axis=1)
        rows.append(r)
    return jnp.concatenate(rows, axis=0)


def _dsa_kernel(qt_ref, kt_ref, q_ref, k_ref, v_ref, km_ref, vm_ref, mask_ref, maskm_ref, band_ref, o_ref,
                m_ref, l_ref, acc_ref, *, nh, hd):
    step = pl.program_id(1)
    qt, kt = qt_ref[step], kt_ref[step]
    tq, ts = q_ref.shape[1], k_ref.shape[1]
    scale = hd ** -0.5 * LOG2E

    @pl.when(kt == 0)
    def _():
        _attn_init(m_ref, l_ref, acc_ref)

    def off_diagonal(near):
        maskf = mask_ref[0].astype(F32)
        for h in range(nh):
            hs = slice(h * hd, (h + 1) * hd)
            s = _dot_nt(q_ref[0, :, hs], k_ref[0, :, hs]) * scale + maskf
            if near:
                s = _add_blocks(s, {(0, ts // LANES - 1): band_ref[1, h]})
            _online_softmax_step(h, hd, s, v_ref[0, :, hs], m_ref, l_ref, acc_ref)

    def diagonal():
        first = (qt == 0).astype(F32)
        maskf = mask_ref[0].astype(F32)
        maskm = maskm_ref[0].astype(F32)
        for h in range(nh):
            hs = slice(h * hd, (h + 1) * hd)
            q = q_ref[0, :, hs]
            s = _dot_nt(q, k_ref[0, :, hs]) * scale + maskf
            sm = _dot_nt(q, km_ref[:, hs]) * scale + maskm
            s = jnp.concatenate([s, sm], axis=1)
            adds = {(0, ts // LANES): band_ref[2, h] * first}
            for ib in range(tq // LANES):
                adds[(ib, ib)] = band_ref[0, h]
                if ib > 0:
                    adds[(ib, ib - 1)] = band_ref[1, h]
            s = _add_blocks(s, adds)
            v = jnp.concatenate([v_ref[0, :, hs], vm_ref[:, hs]], axis=0)
            _online_softmax_step(h, hd, s, v, m_ref, l_ref, acc_ref)

    @pl.when(kt < qt - 1)
    def _():
        off_diagonal(False)

    @pl.when(kt == qt - 1)
    def _():
        off_diagonal(True)

    @pl.when(kt == qt)
    def _():
        diagonal()
        _attn_finish(o_ref, l_ref, acc_ref, nh, hd)


def _attention(kind, big, bigm, nh, hd, tile, extra):
    b, L, _ = big.shape
    w = nh * hd
    q_blk, k_blk, v_blk = 0, 1, 2
    t = _tile(L, tile)
    nq = L // t
    pairs = [(qi, ki) for qi in range(nq) for ki in range(qi + 1)]
    qt_tab = jnp.asarray([p[0] for p in pairs], jnp.int32)
    kt_tab = jnp.asarray([p[1] for p in pairs], jnp.int32)
    q_spec = pl.BlockSpec((1, t, w), lambda bi, s, qt, kt: (bi, qt[s], q_blk))
    kv = lambda blk: pl.BlockSpec((1, t, w), lambda bi, s, qt, kt: (bi, kt[s], blk))
    kvm = lambda blk: pl.BlockSpec((META_PAD, w), lambda bi, s, qt, kt: (0, blk))
    scratch = [
        pltpu.VMEM((nh, t, LANES), F32),
        pltpu.VMEM((nh, t, LANES), F32),
        pltpu.VMEM((t, w), F32),
    ]
    if kind == "fox":
        dt, dtm, n_meta = extra
        body = functools.partial(_fox_kernel, nh=nh, hd=hd, n_meta=n_meta)
        in_specs = [q_spec, kv(k_blk), kv(v_blk), kvm(k_blk), kvm(v_blk),
                    pl.BlockSpec((1, nh, t), lambda bi, s, qt, kt: (bi, 0, kt[s])),
                    pl.BlockSpec((1, nh, META_PAD), lambda bi, s, qt, kt: (bi, 0, 0))]
        args = [big, big, big, bigm, bigm, dt, dtm]
    else:
        mask, band = extra
        body = functools.partial(_dsa_kernel, nh=nh, hd=hd)
        in_specs = [q_spec, kv(k_blk), kv(v_blk), kvm(k_blk), kvm(v_blk),
                    pl.BlockSpec((1, t, t), lambda bi, s, qt, kt: (bi, qt[s], kt[s])),
                    pl.BlockSpec((1, t, META_PAD), lambda bi, s, qt, kt: (bi, qt[s], L // META_PAD)),
                    pl.BlockSpec((3, nh, LANES, LANES), lambda bi, s, qt, kt: (0, 0, 0, 0))]
        args = [big, big, big, bigm, bigm, mask, mask, band]
    return pl.pallas_call(
        body,
        grid_spec=pltpu.PrefetchScalarGridSpec(
            num_scalar_prefetch=2,
            grid=(b, len(pairs)),
            in_specs=in_specs,
            out_specs=pl.BlockSpec((1, t, w), lambda bi, s, qt, kt: (bi, qt[s], 0)),
            scratch_shapes=scratch,
        ),
        out_shape=jax.ShapeDtypeStruct((b, L, w), BF16),
        compiler_params=_cparams("arbitrary", "arbitrary"),
        name=kind + "_attention",
    )(qt_tab, kt_tab, *args)


def _sortable(x):
    bits = pltpu.bitcast(x, jnp.int32)
    return bits ^ ((bits >> 31) & 0x7FFFFFFF)


def _index_kernel(qi_ref, w_ref, kie_ref, kio_ref, kiem_ref, kiom_ref, mask_ref, wb_ref, st_ref, smt_ref,
                  *, tc, w_lane0, n_meta, k_top):
    qt = pl.program_id(1)
    tq = qi_ref.shape[1]
    L = kie_ref.shape[1]
    n_ct = (qt + 1) * tq // tc
    npair = H_IDX // 2
    wscale = (H_IDX ** -0.5) * (D_IDX ** -0.5)

    wv = w_ref[0]
    for h in range(H_IDX):
        col = wv[:, w_lane0 + h:w_lane0 + h + 1] * wscale
        wb_ref[h] = jnp.broadcast_to(col, (tq, LANES))

    def score_tile(ke, ko):
        n = ke.shape[0]
        accs = [jnp.zeros((tq, LANES), F32) for _ in range(n // LANES)]
        for j in range(npair):
            lhs = qi_ref[0, :, j * LANES:(j + 1) * LANES]
            de = jnp.maximum(_dot_nt(lhs, ke), 0.0)
            do = jnp.maximum(_dot_nt(lhs, ko), 0.0)
            for c in range(n // LANES):
                cs = slice(c * LANES, (c + 1) * LANES)
                accs[c] = accs[c] + wb_ref[2 * j] * de[:, cs] + wb_ref[2 * j + 1] * do[:, cs]
        return accs[0] if len(accs) == 1 else jnp.concatenate(accs, axis=1)

    colm = lax.broadcasted_iota(jnp.int32, (tq, META_PAD), 1)
    sm = jnp.where(colm < n_meta, score_tile(kiem_ref[...], kiom_ref[...]), -jnp.inf)
    smt_ref[...] = _sortable(sm.T)

    qchunk = (qt * tq + lax.broadcasted_iota(jnp.int32, (tq, tc), 0)) // CHUNK

    def tile_body(c, carry):
        start = pl.multiple_of(c * tc, tc)
        sc = score_tile(kie_ref[0, pl.ds(start, tc), :], kio_ref[0, pl.ds(start, tc), :])
        kchunk = (start + lax.broadcasted_iota(jnp.int32, (tq, tc), 1)) // CHUNK
        sc = jnp.where(kchunk <= qchunk, sc, -jnp.inf)
        st_ref[c] = _sortable(sc.T)
        return carry

    lax.fori_loop(0, n_ct, tile_body, 0)

    sub = 8
    nacc = 4

    def count_tile(tile, thr8, accs):
        for r in range(tile.shape[0] // sub):
            accs[r % nacc] = accs[r % nacc] + (tile[r * sub:(r + 1) * sub, :] >= thr8).astype(jnp.int32)
        return accs

    def count_ge(thr8):
        accs = count_tile(smt_ref[...], thr8, [jnp.zeros((sub, tq), jnp.int32) for _ in range(nacc)])
        accs = lax.fori_loop(0, n_ct, lambda c, a: tuple(count_tile(st_ref[c], thr8, list(a))), tuple(accs))
        total = functools.reduce(lambda x, y: x + y, accs)
        return jnp.broadcast_to(jnp.sum(total, axis=0, keepdims=True), (sub, tq))

    def fold(x, op):
        acc = x[:sub]
        for r in range(1, x.shape[0] // sub):
            acc = op(acc, x[r * sub:(r + 1) * sub])
        return acc

    gmax = lax.fori_loop(0, n_ct, lambda c, g: jnp.maximum(g, st_ref[c]),
                         jnp.full((tc, tq), KEY_NEG_INF, jnp.int32))
    ub = jnp.max(jnp.maximum(fold(gmax, jnp.maximum), fold(smt_ref[...], jnp.maximum)), axis=0, keepdims=True)
    if tc >= k_top:
        lb = jnp.min(fold(gmax, jnp.minimum), axis=0, keepdims=True)
    else:
        lb = jnp.full((1, tq), KEY_NEG_INF, jnp.int32)
    lb = jnp.broadcast_to(lb, (sub, tq))
    span = jnp.broadcast_to(ub, (sub, tq)) - lb
    exponent = (pltpu.bitcast(span.astype(F32), jnp.int32) >> 23) - 127
    nbits = jnp.where(span < 0, 32, jnp.where(span == 0, 0, exponent + 1))
    n_iter = jnp.minimum(jnp.max(nbits), 32)

    def bit_body(it, d):
        cand = d | (jnp.int32(1) << (n_iter - 1 - it))
        trial = lb + cand
        cnt = count_ge(trial)
        return jnp.where((trial >= lb) & (cnt >= k_top), cand, d)

    d = lax.fori_loop(0, n_iter, bit_body, jnp.zeros((sub, tq), jnp.int32))
    thr8 = jnp.maximum(lb + d, KEY_NEG_INF + 1)
    thr = thr8[:1]

    cnt_ge = count_ge(thr8)
    tied = jnp.max(((cnt_ge > k_top) & (thr8 > KEY_NEG_INF + 1)).astype(jnp.int32)) > 0

    @pl.when(tied)
    def _():
        need = k_top - count_ge(thr8 + 1)[:1]
        posm = lax.broadcasted_iota(jnp.int32, (META_PAD, tq), 0)
        posf = n_meta + lax.broadcasted_iota(jnp.int32, (tc, tq), 0)

        def tied_before(cut):
            acc = jnp.sum(((smt_ref[...] == thr) & (posm < cut)).astype(jnp.int32), axis=0, keepdims=True)

            def body(c, acc):
                hit = (st_ref[c] == thr) & (posf + c * tc < cut)
                return acc + jnp.sum(hit.astype(jnp.int32), axis=0, keepdims=True)

            return lax.fori_loop(0, n_ct, body, acc)

        nbits = (L + META_PAD).bit_length()

        def pos_body(it, cut):
            trial = cut | (jnp.int32(1) << (nbits - 1 - it))
            return jnp.where(tied_before(trial) < need, trial, cut)

        cut = lax.fori_loop(0, nbits, pos_body, jnp.zeros((1, tq), jnp.int32))
        smt_ref[...] = smt_ref[...] - ((smt_ref[...] == thr) & (posm > cut)).astype(jnp.int32)

        def lower(c, carry):
            st_ref[c] = st_ref[c] - ((st_ref[c] == thr) & (posf + c * tc > cut)).astype(jnp.int32)
            return carry

        lax.fori_loop(0, n_ct, lower, 0)

    def mask_of(keys_t):
        return jnp.where(keys_t >= thr, 0.0, NEG).astype(F32).T.astype(mask_ref.dtype)

    mask_ref[0, :, L:] = mask_of(smt_ref[...])
    for c in range(L // tc):
        cols = slice(c * tc, (c + 1) * tc)

        @pl.when(c < n_ct)
        def _():
            mask_ref[0, :, cols] = mask_of(st_ref[c])

        @pl.when(c >= n_ct)
        def _():
            mask_ref[0, :, cols] = jnp.full((tq, tc), NEG, mask_ref.dtype)


def _indexer_mask(qi, slab, kie, kio, kiem, kiom, w_lane0, n_meta, k_top, tq=256, tc=256):
    b, L, _ = qi.shape
    tq = _tile(L, tq)
    tc = _tile(tq, tc)
    wq = H_IDX * D_IDX
    return pl.pallas_call(
        functools.partial(_index_kernel, tc=tc, w_lane0=w_lane0, n_meta=n_meta, k_top=k_top),
        grid=(b, L // tq),
        in_specs=[
            pl.BlockSpec((1, tq, wq), lambda bi, qi: (bi, qi, 0)),
            pl.BlockSpec((1, tq, LANES), lambda bi, qi: (bi, qi, 0)),
            pl.BlockSpec((1, L, LANES), lambda bi, qi: (bi, 0, 0)),
            pl.BlockSpec((1, L, LANES), lambda bi, qi: (bi, 0, 0)),
            pl.BlockSpec((META_PAD, LANES), lambda bi, qi: (0, 0)),
            pl.BlockSpec((META_PAD, LANES), lambda bi, qi: (0, 0)),
        ],
        out_specs=pl.BlockSpec((1, tq, L + META_PAD), lambda bi, qi: (bi, qi, 0)),
        out_shape=jax.ShapeDtypeStruct((b, L, L + META_PAD), BF16),
        scratch_shapes=[
            pltpu.VMEM((H_IDX, tq, LANES), F32),
            pltpu.VMEM((L // tc, tc, tq), jnp.int32),
            pltpu.VMEM((META_PAD, tq), jnp.int32),
        ],
        compiler_params=_cparams("arbitrary", "arbitrary"),
        name="indexer_topk_mask",
    )(qi, slab, kie, kio, kiem, kiom)


def _t5_bucket(rel, n_buckets):
    half = n_buckets // 2
    max_exact = half // 2
    ret = jnp.where(rel > 0, half, 0)
    n = jnp.abs(rel)
    nf = jnp.maximum(n, 1).astype(jnp.float32)
    large = max_exact + (jnp.log(nf / max_exact) / math.log(MAX_DISTANCE / max_exact)
                         * (half - max_exact)).astype(jnp.int32)
    large = jnp.minimum(large, half - 1)
    return ret + jnp.where(n < max_exact, n, large)


def _band_kernel(bucket_ref, far_ref, rb_ref, o_ref, *, n_buckets, nh):
    bucket = bucket_ref[0]
    far = far_ref[0]
    for h in range(nh):
        acc = jnp.zeros(bucket.shape, F32)
        for bk in range(n_buckets):
            acc = jnp.where(bucket == bk, rb_ref[bk, h], acc)
        o_ref[0, h] = (acc - rb_ref[far, h]) * LOG2E


def _band_tables(rel_bias, n_meta):
    n_buckets, nh = rel_bias.shape
    i = jnp.arange(LANES, dtype=jnp.int32)[:, None]
    j = jnp.arange(LANES, dtype=jnp.int32)[None, :]
    rel = jnp.stack([j - i, j - i - LANES, j - n_meta - i])
    bucket = _t5_bucket(rel, n_buckets).astype(jnp.int32)
    far = _t5_bucket(jnp.full((1,), -MAX_DISTANCE, jnp.int32), n_buckets).astype(jnp.int32)
    return pl.pallas_call(
        functools.partial(_band_kernel, n_buckets=n_buckets, nh=nh),
        grid=(3,),
        in_specs=[
            pl.BlockSpec((1, LANES, LANES), lambda t: (t, 0, 0)),
            pl.BlockSpec(memory_space=pltpu.SMEM),
            pl.BlockSpec(memory_space=pltpu.SMEM),
        ],
        out_specs=pl.BlockSpec((1, nh, LANES, LANES), lambda t: (t, 0, 0, 0)),
        out_shape=jax.ShapeDtypeStruct((3, nh, LANES, LANES), F32),
        compiler_params=_cparams("arbitrary"),
        name="rel_bias_band",
    )(bucket, far, rel_bias.astype(F32))


def kernel(x, meta_tokens, attn_norm_g, w_in, forget_bias, rel_bias, w_branch_fox, w_branch_dsa,
           w_out, mlp_norm_g, w_up, w_down, final_norm_g):
    b, L, d = x.shape
    n_meta = meta_tokens.shape[0]
    h_fox = forget_bias.shape[1]
    h_dsa = rel_bias.shape[1]
    w_fox, w_dsa = w_branch_fox.shape[1], w_branch_dsa.shape[1]
    hd = w_fox // h_fox
    w_idx = H_IDX * D_IDX
    assert attn_norm_g.shape[0] == 1, "single-layer block"
    assert w_dsa // h_dsa == hd and hd % LANES == 0
    assert n_meta <= META_PAD and L % CHUNK == 0 and L % META_PAD == 0
    assert MAX_DISTANCE <= LANES and LANES % CHUNK == 0
    k_top = min(TOPK_MAX, L // 4)

    off_fa = 3 * w_fox
    off_dsa = off_fa + h_fox
    off_qi = off_dsa + 3 * w_dsa
    off_ki = off_qi + w_idx
    off_gate = off_ki + D_IDX + H_IDX
    assert off_gate + 2 * d == w_in.shape[2] and D_IDX + H_IDX + h_fox <= LANES
    wt = jnp.swapaxes(w_in, 1, 2)[0]
    w_fox3 = _columns_bf16(wt, [(0, 3 * w_fox)])
    w_dsa3 = _columns_bf16(wt, [(off_dsa, 3 * w_dsa)])
    w_qi = _columns_bf16(wt, [(off_qi, w_idx)])
    w_gate = _columns_bf16(wt, [(off_gate, 2 * d)])
    w_small = _columns_bf16(wt, [(off_ki, D_IDX + H_IDX), (off_fa, h_fox)])
    wi_lane0, fa_lane0 = D_IDX, D_IDX + H_IDX

    x2 = x.reshape(b * L, d)
    meta = jnp.zeros((META_PAD, d), x.dtype).at[:n_meta].set(meta_tokens.astype(x.dtype))
    u = _rmsnorm(x2, attn_norm_g[0], BF16)
    um = _rmsnorm(meta, attn_norm_g[0], BF16)
    fox3 = _mm(u, w_fox3, BF16).reshape(b, L, -1)
    fox3m = _mm(um, w_fox3, BF16)
    dsa3 = _mm(u, w_dsa3, BF16).reshape(b, L, -1)
    dsa3m = _mm(um, w_dsa3, BF16)
    qi, slab, w_out_b, w_bf_b, w_bd_b = _mm(u, w_qi, BF16, narrow=w_small, tn=512,
                                            casts=(w_out[0], w_branch_fox[0], w_branch_dsa[0]))
    qi, slab = qi.reshape(b, L, -1), slab.reshape(b, L, LANES)
    gate, w_up_b = _mm(u, w_gate, F32, act="sigmoid", casts=(w_up[0],))
    slabm = _mm(um, w_small, F32, tn=LANES)

    bias_row = jnp.zeros((1, LANES), F32).at[0, fa_lane0:fa_lane0 + h_fox].set(forget_bias[0].astype(F32))
    dt, dtm = _decay(slab, slabm, bias_row, fa_lane0, h_fox, n_meta)
    o_fox = _attention("fox", fox3, fox3m, h_fox, hd, 512, (dt, dtm, n_meta))

    zpad = jnp.zeros(slab.shape[:-1] + (LANES - D_IDX,), BF16)
    ki = slab[..., :D_IDX].astype(BF16)
    kie, kio = jnp.concatenate([ki, zpad], axis=-1), jnp.concatenate([zpad, ki], axis=-1)
    kim = slabm[:, :D_IDX].astype(BF16)
    kiem = jnp.concatenate([kim, zpad[0, :META_PAD]], axis=-1)
    kiom = jnp.concatenate([zpad[0, :META_PAD], kim], axis=-1)
    mask = _indexer_mask(qi, slab, kie, kio, kiem, kiom, wi_lane0, n_meta, k_top)
    band = _band_tables(rel_bias, n_meta)
    o_dsa = _attention("dsa", dsa3, dsa3m, h_dsa, hd, 512, (mask, band))

    mixed = _branch_merge(o_fox.reshape(b * L, w_fox), o_dsa.reshape(b * L, w_dsa),
                          w_bf_b, w_bd_b, gate, BF16)
    h2 = _mm(mixed, w_out_b, F32, res=x2, tn=512)

    u2 = _rmsnorm(h2, mlp_norm_g[0], BF16)
    a, w_down_b = _mm(u2, w_up_b, BF16, act="relu2", casts=(w_down[0],))
    h3 = _mm_acc(a, w_down_b, h2, F32)
    return _rmsnorm(h3, final_norm_g, x.dtype).reshape(b, L, d)
```

```python
import functools
import math

import jax
import jax.numpy as jnp
from jax import lax
from jax.experimental import pallas as pl
from jax.experimental.pallas import tpu as pltpu

F32 = jnp.float32
BF16 = jnp.bfloat16

CHUNK = 64
H_IDX = 32
D_IDX = 64
TOPK_MAX = 256
MAX_DISTANCE = 128
RMS_EPS = 1e-6

LANES = 128
SUBLANES = 8
BF16_SUBLANES = 16
META_PAD = 128
NEG = -1e30
LOG2E = math.log2(math.e)
INT_MIN = -(2 ** 31)
KEY_NEG_INF = -2139095041
VMEM_LIMIT = 56 * 1024 * 1024
VMEM_LIMIT_BIG = 60 * 1024 * 1024


def _tile(dim, pref):
    if dim <= pref:
        return dim
    for t in range(pref - pref % LANES, 0, -LANES):
        if dim % t == 0:
            return t
    raise ValueError((dim, pref))


def _cparams(*sem, vmem=VMEM_LIMIT):
    return pltpu.CompilerParams(dimension_semantics=sem, vmem_limit_bytes=vmem)


def _rmsnorm_kernel(x_ref, g_ref, o_ref):
    x = x_ref[...].astype(F32)
    ms = jnp.mean(x * x, axis=-1, keepdims=True)
    y = x * lax.rsqrt(ms + RMS_EPS)
    o_ref[...] = (y * g_ref[...]).astype(o_ref.dtype)


def _rmsnorm(x, g, out_dtype):
    m, d = x.shape
    tr = _tile(m, 256)
    return pl.pallas_call(
        _rmsnorm_kernel,
        grid=(m // tr,),
        in_specs=[pl.BlockSpec((tr, d), lambda i: (i, 0)), pl.BlockSpec((1, d), lambda i: (0, 0))],
        out_specs=pl.BlockSpec((tr, d), lambda i: (i, 0)),
        out_shape=jax.ShapeDtypeStruct((m, d), out_dtype),
        compiler_params=_cparams("arbitrary"),
        name="rmsnorm",
    )(x, g.reshape(1, d).astype(F32))


def _columns_kernel(*refs):
    *w_refs, o_ref = refs
    parts = [r[...] for r in w_refs]
    pad = o_ref.shape[1] - sum(p.shape[0] for p in parts)
    if pad:
        parts.append(jnp.zeros((pad, parts[0].shape[1]), parts[0].dtype))
    rows = parts[0] if len(parts) == 1 else jnp.concatenate(parts, axis=0)
    o_ref[...] = rows.T.astype(o_ref.dtype)


def _columns_bf16(wt, ranges, chunk=512):
    n, k = wt.shape
    if len(ranges) == 1 and ranges[0][1] > chunk:
        (lo, width), = ranges
        chunk = _tile(width, chunk)
        steps, widths, out_w = width // chunk, [chunk], chunk
        offs = [lo]
    else:
        steps, widths = 1, [w for _, w in ranges]
        offs = [lo for lo, _ in ranges]
        out_w = -(-sum(widths) // LANES) * LANES
    assert all(o % SUBLANES == 0 and w % SUBLANES == 0 for o, w in zip(offs, widths))
    in_specs = [pl.BlockSpec((pl.Element(w), pl.Element(k)),
                             functools.partial(lambda i, o, w: (pl.multiple_of(o + i * w, SUBLANES), 0), o=o, w=w))
                for o, w in zip(offs, widths)]
    return pl.pallas_call(
        _columns_kernel,
        grid=(steps,),
        in_specs=in_specs,
        out_specs=pl.BlockSpec((k, out_w), lambda i: (0, i)),
        out_shape=jax.ShapeDtypeStruct((k, out_w * steps), BF16),
        compiler_params=_cparams("arbitrary"),
        name="weight_columns",
    )(*([wt] * len(offs)))


def _act(acc, act):
    if act == "sigmoid":
        return 0.5 * jnp.tanh(0.5 * acc) + 0.5
    if act == "relu2":
        r = jnp.maximum(acc, 0.0)
        return r * r
    return acc


def _mm_kernel(*refs, act, has_res, has_narrow, n_cast):
    refs = list(refs)
    a_ref = refs.pop(0)
    b_ref = refs.pop(0)
    r_ref = refs.pop(0) if has_res else None
    bn_ref = refs.pop(0) if has_narrow else None
    cast_in = [refs.pop(0) for _ in range(n_cast)]
    o_ref = refs.pop(0)
    on_ref = refs.pop(0) if has_narrow else None
    cast_out = [refs.pop(0) for _ in range(n_cast)]

    acc = _act(jnp.dot(a_ref[...], b_ref[...], preferred_element_type=F32), act)
    if has_res:
        acc = r_ref[...] + acc
    o_ref[...] = acc.astype(o_ref.dtype)
    if has_narrow:
        @pl.when(pl.program_id(1) == 0)
        def _():
            on_ref[...] = jnp.dot(a_ref[...], bn_ref[...], preferred_element_type=F32).astype(on_ref.dtype)
    for ci, co in zip(cast_in, cast_out):
        co[...] = ci[...].astype(co.dtype)


def _mm(a, b, out_dtype, act=None, res=None, tm=1024, tn=1024, narrow=None, casts=()):
    m, k = a.shape
    _, n = b.shape
    tm, tn = _tile(m, tm), _tile(n, tn)
    ni, nj = m // tm, n // tn
    in_specs = [pl.BlockSpec((tm, k), lambda i, j: (i, 0)), pl.BlockSpec((k, tn), lambda i, j: (0, j))]
    args = [a, b]
    out_specs = [pl.BlockSpec((tm, tn), lambda i, j: (i, j))]
    out_shape = [jax.ShapeDtypeStruct((m, n), out_dtype)]
    if res is not None:
        in_specs.append(pl.BlockSpec((tm, tn), lambda i, j: (i, j)))
        args.append(res)
    if narrow is not None:
        in_specs.append(pl.BlockSpec((k, LANES), lambda i, j: (0, 0)))
        args.append(narrow)
        out_specs.append(pl.BlockSpec((tm, LANES), lambda i, j: (i, 0)))
        out_shape.append(jax.ShapeDtypeStruct((m, LANES), F32))
    riding = []
    for c in casts:
        rows, cols = c.shape
        slab = rows // (ni * nj)
        riding.append(slab * ni * nj == rows and slab % BF16_SUBLANES == 0)
        if riding[-1]:
            spec = pl.BlockSpec((slab, cols), lambda i, j: (i * nj + j, 0))
            in_specs.append(spec)
            args.append(c)
            out_specs.append(spec)
            out_shape.append(jax.ShapeDtypeStruct((rows, cols), BF16))
    outs = list(pl.pallas_call(
        functools.partial(_mm_kernel, act=act, has_res=res is not None, has_narrow=narrow is not None,
                          n_cast=sum(riding)),
        grid=(ni, nj),
        in_specs=in_specs,
        out_specs=out_specs,
        out_shape=out_shape,
        compiler_params=_cparams("arbitrary", "arbitrary", vmem=VMEM_LIMIT_BIG),
        name="matmul",
    )(*args))
    n_main = len(outs) - sum(riding)
    cast_outs = iter(outs[n_main:])
    outs = outs[:n_main] + [next(cast_outs) if r else c.astype(BF16) for r, c in zip(riding, casts)]
    return outs[0] if len(outs) == 1 else tuple(outs)


def _mm_acc_kernel(a_ref, b_ref, r_ref, o_ref):
    kk = pl.program_id(2)
    part = jnp.dot(a_ref[...], b_ref[...], preferred_element_type=F32)

    @pl.when(kk == 0)
    def _():
        o_ref[...] = r_ref[...] + part

    @pl.when(kk > 0)
    def _():
        o_ref[...] += part


def _mm_acc(a, b, res, out_dtype, tm=1024, tn=1024, tk=4096):
    assert out_dtype == F32 and res.dtype == F32
    m, k = a.shape
    _, n = b.shape
    tm, tn, tk = _tile(m, tm), _tile(n, tn), _tile(k, tk)
    return pl.pallas_call(
        _mm_acc_kernel,
        grid=(m // tm, n // tn, k // tk),
        in_specs=[
            pl.BlockSpec((tm, tk), lambda i, j, kk: (i, kk)),
            pl.BlockSpec((tk, tn), lambda i, j, kk: (kk, j)),
            pl.BlockSpec((tm, tn), lambda i, j, kk: (i, j)),
        ],
        out_specs=pl.BlockSpec((tm, tn), lambda i, j, kk: (i, j)),
        out_shape=jax.ShapeDtypeStruct((m, n), out_dtype),
        compiler_params=_cparams("arbitrary", "arbitrary", "arbitrary", vmem=VMEM_LIMIT_BIG),
        name="matmul_acc",
    )(a, b, res)


def _branch_kernel(of_ref, od_ref, wf_ref, wd_ref, g0_ref, g1_ref, o_ref):
    yf = jnp.dot(of_ref[...], wf_ref[...], preferred_element_type=F32)
    yd = jnp.dot(od_ref[...], wd_ref[...], preferred_element_type=F32)
    o_ref[...] = (g0_ref[...] * yf + g1_ref[...] * yd).astype(o_ref.dtype)


def _branch_merge(o_fox, o_dsa, wf, wd, gate, out_dtype, tm=1024, tn=512):
    m, kf = o_fox.shape
    _, kd = o_dsa.shape
    n = wf.shape[1]
    tm, tn = _tile(m, tm), _tile(n, tn)
    nj = n // tn
    return pl.pallas_call(
        _branch_kernel,
        grid=(m // tm, nj),
        in_specs=[
            pl.BlockSpec((tm, kf), lambda i, j: (i, 0)),
            pl.BlockSpec((tm, kd), lambda i, j: (i, 0)),
            pl.BlockSpec((kf, tn), lambda i, j: (0, j)),
            pl.BlockSpec((kd, tn), lambda i, j: (0, j)),
            pl.BlockSpec((tm, tn), lambda i, j: (i, j)),
            pl.BlockSpec((tm, tn), lambda i, j: (i, j + nj)),
        ],
        out_specs=pl.BlockSpec((tm, tn), lambda i, j: (i, j)),
        out_shape=jax.ShapeDtypeStruct((m, n), out_dtype),
        compiler_params=_cparams("arbitrary", "arbitrary"),
        name="branch_merge",
    )(o_fox, o_dsa, wf, wd, gate, gate)


def _split3(x):
    hi = x.astype(BF16)
    r1 = x - hi.astype(F32)
    mid = r1.astype(BF16)
    lo = (r1 - mid.astype(F32)).astype(BF16)
    return hi, mid, lo


def _cumsum_lanes(xt, upper):
    out = None
    for part in _split3(xt):
        d = jnp.dot(part, upper, preferred_element_type=F32)
        out = d if out is None else out + d
    return out


def _decay_kernel(slab_ref, slabm_ref, bias_ref, dt_ref, dtm_ref, *, blk, lane0, nh, n_meta):
    L = slab_ref.shape[1]
    bias = bias_ref[...]

    def upper(n):
        r = lax.broadcasted_iota(jnp.int32, (n, n), 0)
        c = lax.broadcasted_iota(jnp.int32, (n, n), 1)
        return (r <= c).astype(BF16)

    xm = jax.nn.log_sigmoid(slabm_ref[...] + bias)
    dm = _cumsum_lanes(xm.T, upper(META_PAD))
    dtm_ref[0] = dm[lane0:lane0 + nh, :] * -LOG2E
    carry = dm[:, n_meta - 1:n_meta]
    up = upper(blk)
    for i in range(L // blk):
        x = jax.nn.log_sigmoid(slab_ref[0, i * blk:(i + 1) * blk, :] + bias)
        d = _cumsum_lanes(x.T, up) + carry
        dt_ref[0, :, i * blk:(i + 1) * blk] = d[lane0:lane0 + nh, :] * -LOG2E
        carry = d[:, blk - 1:blk]


def _decay(slab, slabm, bias_row, lane0, nh, n_meta):
    b, L, _ = slab.shape
    blk = _tile(L, 256)
    return pl.pallas_call(
        functools.partial(_decay_kernel, blk=blk, lane0=lane0, nh=nh, n_meta=n_meta),
        grid=(b,),
        in_specs=[
            pl.BlockSpec((1, L, LANES), lambda i: (i, 0, 0)),
            pl.BlockSpec((META_PAD, LANES), lambda i: (0, 0)),
            pl.BlockSpec((1, LANES), lambda i: (0, 0)),
        ],
        out_specs=[
            pl.BlockSpec((1, nh, L), lambda i: (i, 0, 0)),
            pl.BlockSpec((1, nh, META_PAD), lambda i: (i, 0, 0)),
        ],
        out_shape=[
            jax.ShapeDtypeStruct((b, nh, L), F32),
            jax.ShapeDtypeStruct((b, nh, META_PAD), F32),
        ],
        compiler_params=_cparams("arbitrary"),
        name="forget_cumsum",
    )(slab, slabm, bias_row)


def _dot_nt(a, b):
    return lax.dot_general(a, b, (((1,), (1,)), ((), ())), preferred_element_type=F32)


def _scaled(q, scale):
    return (q.astype(F32) * scale).astype(q.dtype)


def _online_softmax_step(h, hd, s2, v, m_ref, l_ref, acc_ref):
    ts = s2.shape[1]
    m_prev = m_ref[h]
    m_new = jnp.maximum(m_prev, jnp.max(s2, axis=1, keepdims=True))
    alpha = jnp.exp2(m_prev - m_new)
    p = jnp.exp2(s2 - jnp.tile(m_new, (1, ts // LANES))).astype(BF16)
    v_ones = jnp.concatenate([v, jnp.ones((ts, LANES), v.dtype)], axis=1)
    pv = jnp.dot(p, v_ones, preferred_element_type=F32)
    hs = slice(h * hd, (h + 1) * hd)
    acc_ref[:, hs] = acc_ref[:, hs] * jnp.tile(alpha, (1, hd // LANES)) + pv[:, :hd]
    l_ref[h] = alpha * l_ref[h] + pv[:, hd:]
    m_ref[h] = m_new


def _attn_init(m_ref, l_ref, acc_ref):
    m_ref[...] = jnp.full_like(m_ref, NEG)
    l_ref[...] = jnp.zeros_like(l_ref)
    acc_ref[...] = jnp.zeros_like(acc_ref)


def _attn_finish(o_ref, l_ref, acc_ref, nh, hd):
    for h in range(nh):
        hs = slice(h * hd, (h + 1) * hd)
        o_ref[0, :, hs] = (acc_ref[:, hs] / jnp.tile(l_ref[h], (1, hd // LANES))).astype(o_ref.dtype)


def _fox_kernel(qt_ref, kt_ref, q_ref, k_ref, v_ref, km_ref, vm_ref, dt_ref, dtm_ref, o_ref, m_ref, l_ref, acc_ref,
                *, nh, hd, n_meta):
    step = pl.program_id(1)
    qt, kt = qt_ref[step], kt_ref[step]
    tq, ts = q_ref.shape[1], k_ref.shape[1]
    scale = hd ** -0.5 * LOG2E

    @pl.when(kt == 0)
    def _():
        _attn_init(m_ref, l_ref, acc_ref)

    def off_diagonal():
        for h in range(nh):
            hs = slice(h * hd, (h + 1) * hd)
            s = _dot_nt(_scaled(q_ref[0, :, hs], scale), k_ref[0, :, hs]) + dt_ref[0, h:h + 1, :]
            _online_softmax_step(h, hd, s, v_ref[0, :, hs], m_ref, l_ref, acc_ref)

    def diagonal():
        row = lax.broadcasted_iota(jnp.int32, (tq, ts), 0)
        col = lax.broadcasted_iota(jnp.int32, (tq, ts), 1)
        causal = col <= row
        colm = lax.broadcasted_iota(jnp.int32, (tq, META_PAD), 1)
        pad = jnp.where(colm < n_meta, 0.0, NEG).astype(F32)
        for h in range(nh):
            hs = slice(h * hd, (h + 1) * hd)
            q = _scaled(q_ref[0, :, hs], scale)
            s = _dot_nt(q, k_ref[0, :, hs]) + dt_ref[0, h:h + 1, :]
            sm = _dot_nt(q, km_ref[:, hs]) + dtm_ref[0, h:h + 1, :] + pad
            s = jnp.concatenate([jnp.where(causal, s, NEG), sm], axis=1)
            v = jnp.concatenate([v_ref[0, :, hs], vm_ref[:, hs]], axis=0)
            _online_softmax_step(h, hd, s, v, m_ref, l_ref, acc_ref)

    @pl.when(kt < qt)
    def _():
        off_diagonal()

    @pl.when(kt == qt)
    def _():
        diagonal()
        _attn_finish(o_ref, l_ref, acc_ref, nh, hd)


def _add_blocks(s, adds):
    rows = []
    for ib in range(s.shape[0] // LANES):
        r = s[ib * LANES:(ib + 1) * LANES]
        if any(i == ib for i, _ in adds):
            r = jnp.concatenate(
                [r[:, jb * LANES:(jb + 1) * LANES] + adds[(ib, jb)] if (ib, jb) in adds
                 else r[:, jb * LANES:(jb + 1) * LANES] for jb in range(s.shape[1] // LANES)], axis=1)
        rows.append(r)
    return jnp.concatenate(rows, axis=0)


def _dsa_kernel(qt_ref, kt_ref, q_ref, k_ref, v_ref, km_ref, vm_ref, mask_ref, maskm_ref, band_ref, o_ref,
                m_ref, l_ref, acc_ref, *, nh, hd):
    step = pl.program_id(1)
    qt, kt = qt_ref[step], kt_ref[step]
    tq, ts = q_ref.shape[1], k_ref.shape[1]
    scale = hd ** -0.5 * LOG2E

    @pl.when(kt == 0)
    def _():
        _attn_init(m_ref, l_ref, acc_ref)

    def off_diagonal(near):
        maskf = mask_ref[0].astype(F32)
        for h in range(nh):
            hs = slice(h * hd, (h + 1) * hd)
            s = _dot_nt(_scaled(q_ref[0, :, hs], scale), k_ref[0, :, hs]) + maskf
            if near:
                s = _add_blocks(s, {(0, ts // LANES - 1): band_ref[1, h]})
            _online_softmax_step(h, hd, s, v_ref[0, :, hs], m_ref, l_ref, acc_ref)

    def diagonal():
        first = (qt == 0).astype(F32)
        maskf = mask_ref[0].astype(F32)
        maskm = maskm_ref[0].astype(F32)
        for h in range(nh):
            hs = slice(h * hd, (h + 1) * hd)
            q = _scaled(q_ref[0, :, hs], scale)
            s = _dot_nt(q, k_ref[0, :, hs]) + maskf
            sm = _dot_nt(q, km_ref[:, hs]) + maskm
            s = jnp.concatenate([s, sm], axis=1)
            adds = {(0, ts // LANES): band_ref[2, h] * first}
            for ib in range(tq // LANES):
                adds[(ib, ib)] = band_ref[0, h]
                if ib > 0:
                    adds[(ib, ib - 1)] = band_ref[1, h]
            s = _add_blocks(s, adds)
            v = jnp.concatenate([v_ref[0, :, hs], vm_ref[:, hs]], axis=0)
            _online_softmax_step(h, hd, s, v, m_ref, l_ref, acc_ref)

    @pl.when(kt < qt - 1)
    def _():
        off_diagonal(False)

    @pl.when(kt == qt - 1)
    def _():
        off_diagonal(True)

    @pl.when(kt == qt)
    def _():
        diagonal()
        _attn_finish(o_ref, l_ref, acc_ref, nh, hd)


def _attention(kind, big, bigm, nh, hd, tile, extra):
    b, L, _ = big.shape
    w = nh * hd
    q_blk, k_blk, v_blk = 0, 1, 2
    t = _tile(L, tile)
    nq = L // t
    pairs = [(qi, ki) for qi in range(nq) for ki in range(qi + 1)]
    qt_tab = jnp.asarray([p[0] for p in pairs], jnp.int32)
    kt_tab = jnp.asarray([p[1] for p in pairs], jnp.int32)
    q_spec = pl.BlockSpec((1, t, w), lambda bi, s, qt, kt: (bi, qt[s], q_blk))
    kv = lambda blk: pl.BlockSpec((1, t, w), lambda bi, s, qt, kt: (bi, kt[s], blk))
    kvm = lambda blk: pl.BlockSpec((META_PAD, w), lambda bi, s, qt, kt: (0, blk))
    scratch = [
        pltpu.VMEM((nh, t, LANES), F32),
        pltpu.VMEM((nh, t, LANES), F32),
        pltpu.VMEM((t, w), F32),
    ]
    if kind == "fox":
        dt, dtm, n_meta = extra
        body = functools.partial(_fox_kernel, nh=nh, hd=hd, n_meta=n_meta)
        in_specs = [q_spec, kv(k_blk), kv(v_blk), kvm(k_blk), kvm(v_blk),
                    pl.BlockSpec((1, nh, t), lambda bi, s, qt, kt: (bi, 0, kt[s])),
                    pl.BlockSpec((1, nh, META_PAD), lambda bi, s, qt, kt: (bi, 0, 0))]
        args = [big, big, big, bigm, bigm, dt, dtm]
    else:
        mask, band = extra
        body = functools.partial(_dsa_kernel, nh=nh, hd=hd)
        in_specs = [q_spec, kv(k_blk), kv(v_blk), kvm(k_blk), kvm(v_blk),
                    pl.BlockSpec((1, t, t), lambda bi, s, qt, kt: (bi, qt[s], kt[s])),
                    pl.BlockSpec((1, t, META_PAD), lambda bi, s, qt, kt: (bi, qt[s], L // META_PAD)),
                    pl.BlockSpec((3, nh, LANES, LANES), lambda bi, s, qt, kt: (0, 0, 0, 0))]
        args = [big, big, big, bigm, bigm, mask, mask, band]
    return pl.pallas_call(
        body,
        grid_spec=pltpu.PrefetchScalarGridSpec(
            num_scalar_prefetch=2,
            grid=(b, len(pairs)),
            in_specs=in_specs,
            out_specs=pl.BlockSpec((1, t, w), lambda bi, s, qt, kt: (bi, qt[s], 0)),
            scratch_shapes=scratch,
        ),
        out_shape=jax.ShapeDtypeStruct((b, L, w), BF16),
        compiler_params=_cparams("arbitrary", "arbitrary"),
        name=kind + "_attention",
    )(qt_tab, kt_tab, *args)


def _sortable(x):
    bits = pltpu.bitcast(x, jnp.int32)
    return bits ^ ((bits >> 31) & 0x7FFFFFFF)


def _index_kernel(qi_ref, w_ref, kie_ref, kio_ref, kiem_ref, kiom_ref, mask_ref, wb_ref, st_ref, smt_ref,
                  *, tc, w_lane0, n_meta, k_top):
    qt = pl.program_id(1)
    tq = qi_ref.shape[1]
    L = kie_ref.shape[1]
    n_ct = (qt + 1) * tq // tc
    npair = H_IDX // 2
    wscale = (H_IDX ** -0.5) * (D_IDX ** -0.5)

    wv = w_ref[0]
    for h in range(H_IDX):
        col = wv[:, w_lane0 + h:w_lane0 + h + 1] * wscale
        wb_ref[h] = jnp.broadcast_to(col, (tq, LANES))

    def score_tile(ke, ko):
        n = ke.shape[0]
        accs = [jnp.zeros((tq, LANES), F32) for _ in range(n // LANES)]
        for j in range(npair):
            lhs = qi_ref[0, :, j * LANES:(j + 1) * LANES]
            de = jnp.maximum(_dot_nt(lhs, ke), 0.0)
            do = jnp.maximum(_dot_nt(lhs, ko), 0.0)
            for c in range(n // LANES):
                cs = slice(c * LANES, (c + 1) * LANES)
                accs[c] = accs[c] + wb_ref[2 * j] * de[:, cs] + wb_ref[2 * j + 1] * do[:, cs]
        return accs[0] if len(accs) == 1 else jnp.concatenate(accs, axis=1)

    colm = lax.broadcasted_iota(jnp.int32, (tq, META_PAD), 1)
    sm = jnp.where(colm < n_meta, score_tile(kiem_ref[...], kiom_ref[...]), -jnp.inf)
    smt_ref[...] = _sortable(sm.T)

    qchunk = (qt * tq + lax.broadcasted_iota(jnp.int32, (tq, tc), 0)) // CHUNK

    def tile_body(c, carry):
        start = pl.multiple_of(c * tc, tc)
        sc = score_tile(kie_ref[0, pl.ds(start, tc), :], kio_ref[0, pl.ds(start, tc), :])
        kchunk = (start + lax.broadcasted_iota(jnp.int32, (tq, tc), 1)) // CHUNK
        sc = jnp.where(kchunk <= qchunk, sc, -jnp.inf)
        st_ref[c] = _sortable(sc.T)
        return carry

    lax.fori_loop(0, n_ct, tile_body, 0)

    sub = 8
    nacc = 4

    def count_tile(tile, thr8, accs):
        for r in range(tile.shape[0] // sub):
            accs[r % nacc] = accs[r % nacc] + (tile[r * sub:(r + 1) * sub, :] >= thr8).astype(jnp.int32)
        return accs

    def count_ge(thr8):
        accs = count_tile(smt_ref[...], thr8, [jnp.zeros((sub, tq), jnp.int32) for _ in range(nacc)])
        accs = lax.fori_loop(0, n_ct, lambda c, a: tuple(count_tile(st_ref[c], thr8, list(a))), tuple(accs))
        total = functools.reduce(lambda x, y: x + y, accs)
        return jnp.broadcast_to(jnp.sum(total, axis=0, keepdims=True), (sub, tq))

    def bit_body(it, ans_u):
        trial_u = ans_u | (jnp.int32(1) << (31 - it))
        cnt = count_ge(trial_u ^ INT_MIN)
        return jnp.where(cnt >= k_top, trial_u, ans_u)

    ans_u = lax.fori_loop(0, 32, bit_body, jnp.zeros((sub, tq), jnp.int32))
    thr8 = jnp.maximum(ans_u ^ INT_MIN, KEY_NEG_INF + 1)
    thr = thr8[:1]

    cnt_ge = count_ge(thr8)
    tied = jnp.max(((cnt_ge > k_top) & (thr8 > KEY_NEG_INF + 1)).astype(jnp.int32)) > 0

    @pl.when(tied)
    def _():
        need = k_top - count_ge(thr8 + 1)[:1]
        posm = lax.broadcasted_iota(jnp.int32, (META_PAD, tq), 0)
        posf = n_meta + lax.broadcasted_iota(jnp.int32, (tc, tq), 0)

        def tied_before(cut):
            acc = jnp.sum(((smt_ref[...] == thr) & (posm < cut)).astype(jnp.int32), axis=0, keepdims=True)

            def body(c, acc):
                hit = (st_ref[c] == thr) & (posf + c * tc < cut)
                return acc + jnp.sum(hit.astype(jnp.int32), axis=0, keepdims=True)

            return lax.fori_loop(0, n_ct, body, acc)

        nbits = (L + META_PAD).bit_length()

        def pos_body(it, cut):
            trial = cut | (jnp.int32(1) << (nbits - 1 - it))
            return jnp.where(tied_before(trial) < need, trial, cut)

        cut = lax.fori_loop(0, nbits, pos_body, jnp.zeros((1, tq), jnp.int32))
        smt_ref[...] = smt_ref[...] - ((smt_ref[...] == thr) & (posm > cut)).astype(jnp.int32)

        def lower(c, carry):
            st_ref[c] = st_ref[c] - ((st_ref[c] == thr) & (posf + c * tc > cut)).astype(jnp.int32)
            return carry

        lax.fori_loop(0, n_ct, lower, 0)

    def mask_of(keys_t):
        return jnp.where(keys_t >= thr, 0.0, NEG).astype(F32).T.astype(mask_ref.dtype)

    mask_ref[0, :, L:] = mask_of(smt_ref[...])
    for c in range(L // tc):
        cols = slice(c * tc, (c + 1) * tc)

        @pl.when(c < n_ct)
        def _():
            mask_ref[0, :, cols] = mask_of(st_ref[c])

        @pl.when(c >= n_ct)
        def _():
            mask_ref[0, :, cols] = jnp.full((tq, tc), NEG, mask_ref.dtype)


def _indexer_mask(qi, slab, kie, kio, kiem, kiom, w_lane0, n_meta, k_top, tq=256, tc=256):
    b, L, _ = qi.shape
    tq = _tile(L, tq)
    tc = _tile(tq, tc)
    wq = H_IDX * D_IDX
    return pl.pallas_call(
        functools.partial(_index_kernel, tc=tc, w_lane0=w_lane0, n_meta=n_meta, k_top=k_top),
        grid=(b, L // tq),
        in_specs=[
            pl.BlockSpec((1, tq, wq), lambda bi, qi: (bi, qi, 0)),
            pl.BlockSpec((1, tq, LANES), lambda bi, qi: (bi, qi, 0)),
            pl.BlockSpec((1, L, LANES), lambda bi, qi: (bi, 0, 0)),
            pl.BlockSpec((1, L, LANES), lambda bi, qi: (bi, 0, 0)),
            pl.BlockSpec((META_PAD, LANES), lambda bi, qi: (0, 0)),
            pl.BlockSpec((META_PAD, LANES), lambda bi, qi: (0, 0)),
        ],
        out_specs=pl.BlockSpec((1, tq, L + META_PAD), lambda bi, qi: (bi, qi, 0)),
        out_shape=jax.ShapeDtypeStruct((b, L, L + META_PAD), BF16),
        scratch_shapes=[
            pltpu.VMEM((H_IDX, tq, LANES), F32),
            pltpu.VMEM((L // tc, tc, tq), jnp.int32),
            pltpu.VMEM((META_PAD, tq), jnp.int32),
        ],
        compiler_params=_cparams("arbitrary", "arbitrary"),
        name="indexer_topk_mask",
    )(qi, slab, kie, kio, kiem, kiom)


def _t5_bucket(rel, n_buckets):
    half = n_buckets // 2
    max_exact = half // 2
    ret = jnp.where(rel > 0, half, 0)
    n = jnp.abs(rel)
    nf = jnp.maximum(n, 1).astype(jnp.float32)
    large = max_exact + (jnp.log(nf / max_exact) / math.log(MAX_DISTANCE / max_exact)
                         * (half - max_exact)).astype(jnp.int32)
    large = jnp.minimum(large, half - 1)
    return ret + jnp.where(n < max_exact, n, large)


def _band_kernel(bucket_ref, far_ref, rb_ref, o_ref, *, n_buckets, nh):
    bucket = bucket_ref[0]
    far = far_ref[0]
    for h in range(nh):
        acc = jnp.zeros(bucket.shape, F32)
        for bk in range(n_buckets):
            acc = jnp.where(bucket == bk, rb_ref[bk, h], acc)
        o_ref[0, h] = (acc - rb_ref[far, h]) * LOG2E


def _band_tables(rel_bias, n_meta):
    n_buckets, nh = rel_bias.shape
    i = jnp.arange(LANES, dtype=jnp.int32)[:, None]
    j = jnp.arange(LANES, dtype=jnp.int32)[None, :]
    rel = jnp.stack([j - i, j - i - LANES, j - n_meta - i])
    bucket = _t5_bucket(rel, n_buckets).astype(jnp.int32)
    far = _t5_bucket(jnp.full((1,), -MAX_DISTANCE, jnp.int32), n_buckets).astype(jnp.int32)
    return pl.pallas_call(
        functools.partial(_band_kernel, n_buckets=n_buckets, nh=nh),
        grid=(3,),
        in_specs=[
            pl.BlockSpec((1, LANES, LANES), lambda t: (t, 0, 0)),
            pl.BlockSpec(memory_space=pltpu.SMEM),
            pl.BlockSpec(memory_space=pltpu.SMEM),
        ],
        out_specs=pl.BlockSpec((1, nh, LANES, LANES), lambda t: (t, 0, 0, 0)),
        out_shape=jax.ShapeDtypeStruct((3, nh, LANES, LANES), F32),
        compiler_params=_cparams("arbitrary"),
        name="rel_bias_band",
    )(bucket, far, rel_bias.astype(F32))


def kernel(x, meta_tokens, attn_norm_g, w_in, forget_bias, rel_bias, w_branch_fox, w_branch_dsa,
           w_out, mlp_norm_g, w_up, w_down, final_norm_g):
    b, L, d = x.shape
    n_meta = meta_tokens.shape[0]
    h_fox = forget_bias.shape[1]
    h_dsa = rel_bias.shape[1]
    w_fox, w_dsa = w_branch_fox.shape[1], w_branch_dsa.shape[1]
    hd = w_fox // h_fox
    w_idx = H_IDX * D_IDX
    assert attn_norm_g.shape[0] == 1, "single-layer block"
    assert w_dsa // h_dsa == hd and hd % LANES == 0
    assert n_meta <= META_PAD and L % CHUNK == 0 and L % META_PAD == 0
    assert MAX_DISTANCE <= LANES and LANES % CHUNK == 0
    k_top = min(TOPK_MAX, L // 4)

    off_fa = 3 * w_fox
    off_dsa = off_fa + h_fox
    off_qi = off_dsa + 3 * w_dsa
    off_ki = off_qi + w_idx
    off_gate = off_ki + D_IDX + H_IDX
    assert off_gate + 2 * d == w_in.shape[2] and D_IDX + H_IDX + h_fox <= LANES
    wt = jnp.swapaxes(w_in, 1, 2)[0]
    w_fox3 = _columns_bf16(wt, [(0, 3 * w_fox)])
    w_dsa3 = _columns_bf16(wt, [(off_dsa, 3 * w_dsa)])
    w_qi = _columns_bf16(wt, [(off_qi, w_idx)])
    w_gate = _columns_bf16(wt, [(off_gate, 2 * d)])
    w_small = _columns_bf16(wt, [(off_ki, D_IDX + H_IDX), (off_fa, h_fox)])
    wi_lane0, fa_lane0 = D_IDX, D_IDX + H_IDX

    x2 = x.reshape(b * L, d)
    meta = jnp.zeros((META_PAD, d), x.dtype).at[:n_meta].set(meta_tokens.astype(x.dtype))
    u = _rmsnorm(x2, attn_norm_g[0], BF16)
    um = _rmsnorm(meta, attn_norm_g[0], BF16)
    fox3 = _mm(u, w_fox3, BF16).reshape(b, L, -1)
    fox3m = _mm(um, w_fox3, BF16)
    dsa3 = _mm(u, w_dsa3, BF16).reshape(b, L, -1)
    dsa3m = _mm(um, w_dsa3, BF16)
    qi, slab, w_out_b, w_bf_b, w_bd_b = _mm(u, w_qi, BF16, narrow=w_small, tn=512,
                                            casts=(w_out[0], w_branch_fox[0], w_branch_dsa[0]))
    qi, slab = qi.reshape(b, L, -1), slab.reshape(b, L, LANES)
    gate, w_up_b = _mm(u, w_gate, F32, act="sigmoid", casts=(w_up[0],))
    slabm = _mm(um, w_small, F32, tn=LANES)

    bias_row = jnp.zeros((1, LANES), F32).at[0, fa_lane0:fa_lane0 + h_fox].set(forget_bias[0].astype(F32))
    dt, dtm = _decay(slab, slabm, bias_row, fa_lane0, h_fox, n_meta)
    o_fox = _attention("fox", fox3, fox3m, h_fox, hd, 512, (dt, dtm, n_meta))

    zpad = jnp.zeros(slab.shape[:-1] + (LANES - D_IDX,), BF16)
    ki = slab[..., :D_IDX].astype(BF16)
    kie, kio = jnp.concatenate([ki, zpad], axis=-1), jnp.concatenate([zpad, ki], axis=-1)
    kim = slabm[:, :D_IDX].astype(BF16)
    kiem = jnp.concatenate([kim, zpad[0, :META_PAD]], axis=-1)
    kiom = jnp.concatenate([zpad[0, :META_PAD], kim], axis=-1)
    mask = _indexer_mask(qi, slab, kie, kio, kiem, kiom, wi_lane0, n_meta, k_top)
    band = _band_tables(rel_bias, n_meta)
    o_dsa = _attention("dsa", dsa3, dsa3m, h_dsa, hd, 512, (mask, band))

    mixed = _branch_merge(o_fox.reshape(b * L, w_fox), o_dsa.reshape(b * L, w_dsa),
                          w_bf_b, w_bd_b, gate, BF16)
    h2 = _mm(mixed, w_out_b, F32, res=x2, tn=512)

    u2 = _rmsnorm(h2, mlp_norm_g[0], BF16)
    a, w_down_b = _mm(u2, w_up_b, BF16, act="relu2", casts=(w_down[0],))
    h3 = _mm_acc(a, w_down_b, h2, F32)
    return _rmsnorm(h3, final_norm_g, x.dtype).reshape(b, L, d)
```
